```python
import math
import jax, jax.numpy as jnp
from jax import lax
import numpy as np

D_MODEL = 1024
BATCH = 16
SEQ = 2048
DEPTH = 1

MEM_LEN = 256
EPS = 1e-6
ROPE_THETA = 10000.0
NEG = -1e30
ATT_HEAD_DIM = 64
ATT_SLOTS = 8
DIL_GROUPS = ((128, 1), (512, 4), (2048, 16))
N_GROUPS = 3
ATT_HEADS = ATT_SLOTS * N_GROUPS
ATT_QKV = ATT_HEADS * ATT_HEAD_DIM
ATT_OUT = ATT_SLOTS * ATT_HEAD_DIM
ML_HEADS = 4
ML_HEAD_DIM = 128
ML_WIDTH = ML_HEADS * ML_HEAD_DIM
ML_GATES = 4 * ML_HEADS
ML_CHUNK = 128
ML_CONV = 5
CX_HEADS = 4
CX_HEAD_DIM = 128
CX_WIDTH = CX_HEADS * CX_HEAD_DIM
N_BRANCH = 3
N_IN = 3 * ATT_QKV + 4 * ML_WIDTH + ML_GATES + CX_WIDTH + N_BRANCH * D_MODEL
D_FF = 2816

kernel_name = "hybrid_dilated_mlstm_memxattn_block"


def rmsnorm(x, g):
    xf = x.astype(jnp.float32)
    y = xf * lax.rsqrt(jnp.mean(xf * xf, axis=-1, keepdims=True) + EPS)
    return (y * g.astype(jnp.float32)).astype(x.dtype)


def swiglu(h, w_in, w_out):
    gate, up = jnp.split(h @ w_in, 2, axis=-1)
    return (jax.nn.silu(gate) * up) @ w_out


def rope(x):
    s, d = x.shape[1], x.shape[-1]
    inv = ROPE_THETA ** (-jnp.arange(0, d, 2, dtype=jnp.float32) / d)
    ang = jnp.arange(s, dtype=jnp.float32)[:, None] * inv[None, :]
    cos = jnp.cos(ang)[None, :, None, :]
    sin = jnp.sin(ang)[None, :, None, :]
    xf = x.astype(jnp.float32)
    x1, x2 = jnp.split(xf, 2, axis=-1)
    return jnp.concatenate([x1 * cos - x2 * sin, x2 * cos + x1 * sin], axis=-1).astype(x.dtype)


def banded_attention(q, k, v, radius):
    blk = radius
    n, d = q.shape[-2], q.shape[-1]
    lead = q.shape[:-2]
    nb = -(-n // blk)
    npad = nb * blk
    zl = [(0, 0)] * len(lead)
    qb = jnp.pad(q, zl + [(0, npad - n), (0, 0)]).reshape(*lead, nb, blk, d)
    kp = jnp.pad(k, zl + [(blk, npad - n + blk), (0, 0)]).reshape(*lead, nb + 2, blk, d)
    vp = jnp.pad(v, zl + [(blk, npad - n + blk), (0, 0)]).reshape(*lead, nb + 2, blk, d)
    kw = jnp.concatenate([kp[..., 0:nb, :, :], kp[..., 1:nb + 1, :, :], kp[..., 2:nb + 2, :, :]], axis=-2)
    vw = jnp.concatenate([vp[..., 0:nb, :, :], vp[..., 1:nb + 1, :, :], vp[..., 2:nb + 2, :, :]], axis=-2)
    s = jnp.einsum('...nqd,...nkd->...nqk', qb, kw).astype(jnp.float32)
    qpos = jnp.arange(nb)[:, None] * blk + jnp.arange(blk)[None, :]
    kpos = jnp.arange(nb)[:, None] * blk - blk + jnp.arange(3 * blk)[None, :]
    valid = ((jnp.abs(qpos[:, :, None] - kpos[:, None, :]) <= radius)
             & (kpos[:, None, :] >= 0) & (kpos[:, None, :] < n))
    s = jnp.where(valid, s, NEG)
    m = jnp.max(s, axis=-1, keepdims=True)
    p = jnp.exp(s - m)
    l = jnp.sum(p, axis=-1)
    out = jnp.einsum('...nqk,...nkd->...nqd', p.astype(v.dtype), vw).astype(jnp.float32) / l[..., None]
    lse = m[..., 0] + jnp.log(l)
    out = out.reshape(*lead, npad, d)[..., :n, :]
    lse = lse.reshape(*lead, npad)[..., :n]
    return out, lse


def dilated_attention(q, k, v):
    b, s, _, d = q.shape
    outs, lses = [], []
    for g, (w, r) in enumerate(DIL_GROUPS):
        sl = slice(g * ATT_SLOTS, (g + 1) * ATT_SLOTS)

        def to_res(t):
            return t[:, :, sl].reshape(b, s // r, r, ATT_SLOTS, d).transpose(0, 3, 2, 1, 4)

        o, lse = banded_attention(to_res(q), to_res(k), to_res(v), (w // 2) // r)
        outs.append(o.transpose(0, 3, 2, 1, 4).reshape(b, s, ATT_SLOTS, d))
        lses.append(lse.transpose(0, 3, 2, 1).reshape(b, s, ATT_SLOTS))
    alpha = jax.nn.softmax(jnp.stack(lses, axis=0), axis=0)
    out = jnp.sum(alpha[..., None] * jnp.stack(outs, axis=0), axis=0)
    return out.reshape(b, s, ATT_OUT)


def centred_dwconv(x, w, bias):
    c = x.shape[-1]
    y = lax.conv_general_dilated(x, w[:, None, :].astype(x.dtype), window_strides=(1,), padding='SAME',
                                 dimension_numbers=('NWC', 'WIO', 'NWC'), feature_group_count=c)
    return y + bias.astype(x.dtype)


def mlstm_chunkwise(q, k, v, log_i, log_f):
    b, h, s, d = q.shape
    L = ML_CHUNK
    nc = s // L
    chunk = lambda t: jnp.moveaxis(t.reshape(b, h, nc, L, *t.shape[3:]), 2, 0)
    tril = jnp.tril(jnp.ones((L, L), dtype=bool))

    def step(carry, xs):
        C, n, m = carry
        qc, kc, vc, li, lf = xs
        bcum = jnp.cumsum(lf, axis=-1)
        gtot = bcum[..., -1]
        dmat = jnp.where(tril, bcum[..., :, None] - bcum[..., None, :] + li[..., None, :], NEG)
        inter = bcum + m[..., None]
        mt = jnp.maximum(jnp.max(dmat, axis=-1), inter)
        wts = jnp.exp(dmat - mt[..., None])
        w_inter = jnp.exp(inter - mt)
        sc = jnp.einsum('bhtd,bhsd->bhts', qc, kc) * wts
        num = jnp.einsum('bhts,bhsd->bhtd', sc, vc) + w_inter[..., None] * jnp.einsum('bhvk,bhtk->bhtv', C, qc)
        den = jnp.sum(sc, axis=-1) + w_inter * jnp.einsum('bhk,bhtk->bht', n, qc)
        hout = num / jnp.maximum(jnp.abs(den), jnp.exp(-mt))[..., None]
        a = gtot[..., None] - bcum + li
        m_new = jnp.maximum(gtot + m, jnp.max(a, axis=-1))
        decay = jnp.exp(gtot + m - m_new)
        wk = jnp.exp(a - m_new[..., None])
        C_new = decay[..., None, None] * C + jnp.einsum('bhs,bhsv,bhsk->bhvk', wk, vc, kc)
        n_new = decay[..., None] * n + jnp.einsum('bhs,bhsk->bhk', wk, kc)
        return (C_new, n_new, m_new), hout

    init = (jnp.zeros((b, h, d, d), jnp.float32), jnp.zeros((b, h, d), jnp.float32),
            jnp.zeros((b, h), jnp.float32))
    _, hs = lax.scan(step, init, (chunk(q), chunk(k), chunk(v), chunk(log_i), chunk(log_f)))
    return jnp.moveaxis(hs, 0, 2).reshape(b, h, s, d)


def setup_inputs(seed: int = 0) -> dict:
    key = jax.random.key(seed)
    ks = iter(jax.random.split(key, 32))
    f32 = jnp.float32
    nrm = lambda shape, fan_in: jax.random.normal(next(ks), shape, f32) * (fan_in ** -0.5)
    gain = lambda shape: 1.0 + 0.02 * jax.random.normal(next(ks), shape, f32)
    small = lambda shape: 0.02 * jax.random.normal(next(ks), shape, f32)
    x = jax.random.normal(next(ks), (BATCH, SEQ, D_MODEL), f32)
    mem = jax.random.normal(next(ks), (BATCH, MEM_LEN, D_MODEL), f32)
    norm_ffn1 = gain((DEPTH, D_MODEL))
    w_ffn1_in = nrm((DEPTH, D_MODEL, 2 * D_FF), D_MODEL)
    w_ffn1_out = nrm((DEPTH, D_FF, D_MODEL), D_FF)
    norm_mix = gain((DEPTH, D_MODEL))
    norm_mem = gain((DEPTH, D_MODEL))
    w_in = nrm((DEPTH, D_MODEL, N_IN), D_MODEL)
    att_q_gain = gain((DEPTH, ATT_HEAD_DIM))
    att_k_gain = gain((DEPTH, ATT_HEAD_DIM))
    ml_conv_w = nrm((DEPTH, ML_CONV, 2 * ML_WIDTH), ML_CONV)
    ml_conv_b = small((DEPTH, 2 * ML_WIDTH))
    forget_offset = (jnp.array([0.0, 1.0, 0.0, 1.0], f32)[:, None]
                     * jnp.linspace(3.0, 6.0, ML_HEADS, dtype=f32)[None, :])
    ml_gate_b = (0.1 * jax.random.normal(next(ks), (DEPTH, 4, ML_HEADS), f32)
                 + forget_offset[None]).reshape(DEPTH, ML_GATES)
    ml_out_gain = gain((DEPTH, ML_WIDTH))
    cx_q_gain = gain((DEPTH, CX_HEAD_DIM))
    cx_k_gain = gain((DEPTH, CX_HEAD_DIM))
    w_mem_kv = nrm((DEPTH, D_MODEL, 2 * CX_WIDTH), D_MODEL)
    mix_gate_b = small((DEPTH, N_BRANCH * D_MODEL))
    w_br_att = nrm((DEPTH, ATT_OUT, D_MODEL), ATT_OUT)
    w_br_ml = nrm((DEPTH, ML_WIDTH, D_MODEL), ML_WIDTH)
    w_br_cx = nrm((DEPTH, CX_WIDTH, D_MODEL), CX_WIDTH)
    w_out = nrm((DEPTH, D_MODEL, D_MODEL), D_MODEL)
    norm_ffn2 = gain((DEPTH, D_MODEL))
    w_ffn2_in = nrm((DEPTH, D_MODEL, 2 * D_FF), D_MODEL)
    w_ffn2_out = nrm((DEPTH, D_FF, D_MODEL), D_FF)
    norm_final = gain((DEPTH, D_MODEL))
    return {"x": x, "mem": mem, "norm_ffn1": norm_ffn1, "w_ffn1_in": w_ffn1_in, "w_ffn1_out": w_ffn1_out,
            "norm_mix": norm_mix, "norm_mem": norm_mem, "w_in": w_in, "att_q_gain": att_q_gain,
            "att_k_gain": att_k_gain, "ml_conv_w": ml_conv_w, "ml_conv_b": ml_conv_b, "ml_gate_b": ml_gate_b,
            "ml_out_gain": ml_out_gain, "cx_q_gain": cx_q_gain, "cx_k_gain": cx_k_gain, "w_mem_kv": w_mem_kv,
            "mix_gate_b": mix_gate_b, "w_br_att": w_br_att, "w_br_ml": w_br_ml, "w_br_cx": w_br_cx,
            "w_out": w_out, "norm_ffn2": norm_ffn2, "w_ffn2_in": w_ffn2_in, "w_ffn2_out": w_ffn2_out,
            "norm_final": norm_final}


def reference(x, mem, norm_ffn1, w_ffn1_in, w_ffn1_out, norm_mix, norm_mem, w_in, att_q_gain, att_k_gain,
              ml_conv_w, ml_conv_b, ml_gate_b, ml_out_gain, cx_q_gain, cx_k_gain, w_mem_kv, mix_gate_b,
              w_br_att, w_br_ml, w_br_cx, w_out, norm_ffn2, w_ffn2_in, w_ffn2_out, norm_final):
    b, s, dm = x.shape
    mlen = mem.shape[1]
    sizes = (ATT_QKV, ATT_QKV, ATT_QKV, ML_WIDTH, ML_WIDTH, ML_WIDTH, ML_WIDTH, ML_GATES, CX_WIDTH,
             N_BRANCH * D_MODEL)
    split_pts = np.cumsum(np.array(sizes))[:-1].tolist()
    for l in range(DEPTH):
        x = x + 0.5 * swiglu(rmsnorm(x, norm_ffn1[l]), w_ffn1_in[l], w_ffn1_out[l])
        h = rmsnorm(x, norm_mix[l])
        aq, ak, av, mq, mk, mv, mo, mif, cq, gts = jnp.split(h @ w_in[l], split_pts, axis=-1)

        ah = lambda t: t.reshape(b, s, ATT_HEADS, ATT_HEAD_DIM)
        aq = rope(rmsnorm(ah(aq), att_q_gain[l])) * (ATT_HEAD_DIM ** -0.5)
        ak = rope(rmsnorm(ah(ak), att_k_gain[l]))
        y_att = dilated_attention(aq, ak, ah(av)).astype(x.dtype)

        mqk = jax.nn.silu(centred_dwconv(jnp.concatenate([mq, mk], axis=-1), ml_conv_w[l], ml_conv_b[l]))
        mq, mk = jnp.split(mqk, 2, axis=-1)
        mh = lambda t: t.reshape(b, s, ML_HEADS, ML_HEAD_DIM).transpose(0, 2, 1, 3).astype(jnp.float32)
        q_m = mh(mq) * (ML_HEAD_DIM ** -0.5)
        k_m, v_m = mh(mk), mh(mv)
        gates = (mif + ml_gate_b[l]).astype(jnp.float32).reshape(b, s, 4, ML_HEADS).transpose(2, 0, 3, 1)
        li_f, fp_f, li_b, fp_b = gates[0], gates[1], gates[2], gates[3]
        h_f = mlstm_chunkwise(q_m, k_m, v_m, li_f, jax.nn.log_sigmoid(fp_f))
        fl = lambda t: jnp.flip(t, axis=2)
        h_b = fl(mlstm_chunkwise(fl(q_m), fl(k_m), fl(v_m), fl(li_b), fl(jax.nn.log_sigmoid(fp_b))))
        hm = (h_f + h_b).transpose(0, 2, 1, 3)
        hm = hm * lax.rsqrt(jnp.mean(hm * hm, axis=-1, keepdims=True) + EPS)
        hm = hm.reshape(b, s, ML_WIDTH) * ml_out_gain[l].astype(jnp.float32)
        y_ml = (hm * jax.nn.sigmoid(mo.astype(jnp.float32))).astype(x.dtype)

        cq = rmsnorm(cq.reshape(b, s, CX_HEADS, CX_HEAD_DIM), cx_q_gain[l]) * (CX_HEAD_DIM ** -0.5)
        ck, cv = jnp.split(rmsnorm(mem, norm_mem[l]) @ w_mem_kv[l], 2, axis=-1)
        ck = rmsnorm(ck.reshape(b, mlen, CX_HEADS, CX_HEAD_DIM), cx_k_gain[l])
        cv = cv.reshape(b, mlen, CX_HEADS, CX_HEAD_DIM)
        p = jax.nn.softmax(jnp.einsum('bqhd,bkhd->bhqk', cq, ck).astype(jnp.float32), axis=-1)
        y_cx = jnp.einsum('bhqk,bkhd->bqhd', p.astype(cv.dtype), cv).reshape(b, s, CX_WIDTH).astype(x.dtype)

        g = jax.nn.sigmoid((gts + mix_gate_b[l]).astype(jnp.float32)).astype(x.dtype).reshape(b, s, N_BRANCH, dm)
        merged = (g[:, :, 0] * (y_att @ w_br_att[l]) + g[:, :, 1] * (y_ml @ w_br_ml[l])
                  + g[:, :, 2] * (y_cx @ w_br_cx[l]))
        x = x + merged @ w_out[l]

        x = x + 0.5 * swiglu(rmsnorm(x, norm_ffn2[l]), w_ffn2_in[l], w_ffn2_out[l])
        x = rmsnorm(x, norm_final[l])
    return x
```

```python
import functools

import jax
import jax.numpy as jnp
from jax import lax
from jax.experimental import pallas as pl
from jax.experimental.pallas import tpu as pltpu

F32 = jnp.float32
BF16 = jnp.bfloat16

EPS = 1e-6
ROPE_THETA = 10000.0
NEG = -1e30

ATT_HEAD_DIM = 64
ATT_SLOTS = 8
DIL_GROUPS = ((128, 1), (512, 4), (2048, 16))
ATT_OUT = ATT_SLOTS * ATT_HEAD_DIM
ATT_QKV = len(DIL_GROUPS) * ATT_OUT
ML_HEADS = 4
ML_HEAD_DIM = 128
ML_WIDTH = ML_HEADS * ML_HEAD_DIM
ML_GATES = 4 * ML_HEADS
ML_CHUNK = 128
ML_CONV = 5
CX_HEADS = 4
CX_HEAD_DIM = 128
CX_WIDTH = CX_HEADS * CX_HEAD_DIM
N_BRANCH = 3

LANE = 128
MXU_DIM = 256
VMEM_LIMIT = 56 * 1024 * 1024

LSE_LANES_PER_HEAD = LANE // ATT_SLOTS
BAND = 64
assert all((w // 2) // r == BAND for w, r in DIL_GROUPS)

NT_DIMS = (((1,), (1,)), ((), ()))
TN_DIMS = (((0,), (0,)), ((), ()))


def _dot(a, b):
    return jnp.dot(a, b, preferred_element_type=F32)


def _dot_nt(a, b):
    return lax.dot_general(a, b, NT_DIMS, preferred_element_type=F32)


def _dot_exact(a, b):
    return jnp.dot(a, b, preferred_element_type=F32, precision=lax.Precision.HIGHEST)


def _rms(x):
    return x * lax.rsqrt(jnp.mean(x * x, axis=-1, keepdims=True) + EPS)


def _sigmoid(x):
    return 1.0 / (1.0 + jnp.exp(-x))


def _log_sigmoid(x):
    return jnp.minimum(x, 0.0) - jnp.log(1.0 + jnp.exp(-jnp.abs(x)))


def _params(*sem):
    return pltpu.CompilerParams(dimension_semantics=sem, vmem_limit_bytes=VMEM_LIMIT)


def _resident(shape):
    return pl.BlockSpec(shape, lambda *_: (0,) * len(shape), pipeline_mode=pl.Buffered(1))


def _ffn_body(x_ref, g_ref, wg_ref, wu_ref, wo_ref, gf_ref, o_ref, h_ref, acc_ref, *, final_norm):
    j = pl.program_id(1)

    @pl.when(j == 0)
    def _():
        h_ref[...] = (_rms(x_ref[...]) * g_ref[...]).astype(BF16)
        acc_ref[...] = jnp.zeros_like(acc_ref)

    h = h_ref[...]
    gate = _dot(h, wg_ref[...])
    up = _dot(h, wu_ref[...])
    a = (gate * _sigmoid(gate) * up).astype(BF16)
    acc_ref[...] += _dot(a, wo_ref[...])

    @pl.when(j == pl.num_programs(1) - 1)
    def _():
        y = x_ref[...] + 0.5 * acc_ref[...]
        if final_norm:
            y = _rms(y) * gf_ref[...]
        o_ref[...] = y


def _ffn(x, gain, w_in, w_out, final_gain, *, final_norm, tm=512, tf=MXU_DIM):
    t, dm = x.shape
    d_ff = w_out.shape[0]
    nj = d_ff // tf
    return pl.pallas_call(
        functools.partial(_ffn_body, final_norm=final_norm),
        grid=(t // tm, nj),
        in_specs=[
            pl.BlockSpec((tm, dm), lambda i, j: (i, 0)),
            pl.BlockSpec((1, dm), lambda i, j: (0, 0)),
            pl.BlockSpec((dm, tf), lambda i, j: (0, j)),
            pl.BlockSpec((dm, tf), lambda i, j: (0, j + nj)),
            pl.BlockSpec((tf, dm), lambda i, j: (j, 0)),
            pl.BlockSpec((1, dm), lambda i, j: (0, 0)),
        ],
        out_specs=pl.BlockSpec((tm, dm), lambda i, j: (i, 0)),
        out_shape=jax.ShapeDtypeStruct((t, dm), F32),
        scratch_shapes=[pltpu.VMEM((tm, dm), BF16), pltpu.VMEM((tm, dm), F32)],
        compiler_params=_params("parallel", "arbitrary"),
        name="ffn_final" if final_norm else "ffn",
    )(x, gain, w_in, w_in, w_out, final_gain)


def _qkv_body(x_ref, gn_ref, wq_ref, wk_ref, wv_ref, gq_ref, gk_ref, cos_ref, sin_ref, e_ref, *outs):
    q_outs, k_outs, v_outs = outs[0:3], outs[3:6], outs[6:9]
    h = (_rms(x_ref[...]) * gn_ref[...]).astype(BF16)
    cos = cos_ref[...]
    sin = sin_ref[...]
    lane = lax.broadcasted_iota(jnp.int32, cos.shape, 1)
    first_half = (lane & (ATT_HEAD_DIM - 1)) < ATT_HEAD_DIM // 2
    e = e_ref[...]
    for g in range(len(DIL_GROUPS)):
        cs = slice(g * ATT_OUT, (g + 1) * ATT_OUT)
        for w_ref, gain_ref, scale, out in ((wq_ref, gq_ref, ATT_HEAD_DIM ** -0.5, q_outs[g]),
                                            (wk_ref, gk_ref, 1.0, k_outs[g])):
            t = _dot(h, w_ref[:, cs])
            sq = (t * t).astype(BF16)
            ms = jnp.concatenate([_dot(sq[:, c:c + MXU_DIM], e) for c in range(0, ATT_OUT, MXU_DIM)], axis=1)
            tn = t * lax.rsqrt(ms + EPS) * gain_ref[...]
            partner = jnp.where(first_half,
                                pltpu.roll(tn, ATT_OUT - ATT_HEAD_DIM // 2, 1),
                                pltpu.roll(tn, ATT_HEAD_DIM // 2, 1))
            out[...] = ((tn * cos + partner * sin) * scale).astype(BF16)
        v_outs[g][...] = _dot(h, wv_ref[:, cs]).astype(BF16)


def _qkv_proj(x, gn, wq, wk, wv, gq, gk, cos_t, sin_t, e64, *, seq, tm=256):
    t, dm = x.shape
    nseq = seq // tm
    tok = lambda w: pl.BlockSpec((tm, w), lambda i: (i, 0))
    pos = pl.BlockSpec((tm, ATT_OUT), lambda i: (i % nseq, 0))
    out_sd = jax.ShapeDtypeStruct((t, ATT_OUT), BF16)
    return pl.pallas_call(
        _qkv_body,
        grid=(t // tm,),
        in_specs=[tok(dm), _resident((1, dm)), _resident((dm, ATT_QKV)), _resident((dm, ATT_QKV)),
                  _resident((dm, ATT_QKV)), _resident((1, ATT_OUT)), _resident((1, ATT_OUT)), pos, pos,
                  _resident((MXU_DIM, MXU_DIM))],
        out_specs=[tok(ATT_OUT)] * 9,
        out_shape=[out_sd] * 9,
        compiler_params=_params("parallel"),
        name="qkv_proj",
    )(x, gn, wq, wk, wv, gq, gk, cos_t, sin_t, e64)


def _misc_body(x_ref, gn_ref, wqk_ref, wv_ref, wo_ref, wif_ref, bif_ref, wcq_ref, gcq_ref, wg_ref, bg_ref,
               mqk_ref, mv_ref, og_ref, mif_ref, cq_ref, g_ref):
    h = (_rms(x_ref[...]) * gn_ref[...]).astype(BF16)
    mqk_ref[...] = _dot(h, wqk_ref[...]).astype(BF16)
    mv_ref[...] = _dot(h, wv_ref[...]).astype(BF16)
    og_ref[...] = _sigmoid(_dot(h, wo_ref[...])).astype(BF16)
    mif_ref[...] = _dot(h, wif_ref[...]) + bif_ref[...]
    t = _dot(h, wcq_ref[...])
    for hd in range(CX_HEADS):
        cs = slice(hd * CX_HEAD_DIM, (hd + 1) * CX_HEAD_DIM)
        cq_ref[:, cs] = (_rms(t[:, cs]) * gcq_ref[...] * CX_HEAD_DIM ** -0.5).astype(BF16)
    dm = x_ref.shape[1]
    for br in range(N_BRANCH):
        cs = slice(br * dm, (br + 1) * dm)
        g_ref[:, cs] = _sigmoid(_dot(h, wg_ref[:, cs]) + bg_ref[:, cs]).astype(BF16)


def _misc_proj(x, gn, wqk, wv, wo, wif, bif, wcq, gcq, wg, bg, *, tm=256):
    t, dm = x.shape
    tok = lambda w: pl.BlockSpec((tm, w), lambda i: (i, 0))
    widths = (2 * ML_WIDTH, ML_WIDTH, ML_WIDTH, LANE, CX_WIDTH, N_BRANCH * dm)
    dtypes = (BF16, BF16, BF16, F32, BF16, BF16)
    return pl.pallas_call(
        _misc_body,
        grid=(t // tm,),
        in_specs=[tok(dm), _resident((1, dm)), _resident(wqk.shape), _resident(wv.shape), _resident(wo.shape),
                  _resident(wif.shape), _resident(bif.shape), _resident(wcq.shape), _resident(gcq.shape),
                  _resident(wg.shape), _resident(bg.shape)],
        out_specs=[tok(w) for w in widths],
        out_shape=[jax.ShapeDtypeStruct((t, w), d) for w, d in zip(widths, dtypes)],
        compiler_params=_params("parallel"),
        name="misc_proj",
    )(x, gn, wqk, wv, wo, wif, bif, wcq, gcq, wg, bg)


def _attn_body(q_ref, k_ref, v_ref, o_ref, lse_ref, *, n, bq, win):
    lane = lax.broadcasted_iota(jnp.int32, (bq, LANE), 1)
    low_head = lane < ATT_HEAD_DIM
    lse_group = lane // LSE_LANES_PER_HEAD
    qi = lax.broadcasted_iota(jnp.int32, (bq, win), 0)
    ki = lax.broadcasted_iota(jnp.int32, (bq, win), 1)

    def block(q0, ks):
        valid = jnp.abs((q0 + qi) - (ks + ki)) <= BAND
        lse_row = jnp.zeros((bq, LANE), F32)
        for pair in range(ATT_SLOTS // 2):
            cs = slice(pair * LANE, (pair + 1) * LANE)
            qp = q_ref[0, pl.ds(q0, bq), cs]
            kp = k_ref[0, pl.ds(ks, win), cs]
            vp = v_ref[0, pl.ds(ks, win), cs]
            halves = []
            for half in range(2):
                qm = jnp.where(low_head if half == 0 else ~low_head, qp, jnp.zeros_like(qp))
                s = jnp.where(valid, _dot_nt(qm, kp), NEG)
                m = jnp.max(s, axis=-1, keepdims=True)
                p = jnp.exp(s - m)
                l = jnp.sum(p, axis=-1, keepdims=True)
                halves.append(_dot(p.astype(BF16), vp) / l)
                lse_row = jnp.where(lse_group == 2 * pair + half, m + jnp.log(l), lse_row)
            o_ref[0, pl.ds(q0, bq), cs] = jnp.where(low_head, halves[0], halves[1]).astype(BF16)
        lse_ref[0, pl.ds(q0, bq), :] = lse_row

    nblk = n // bq
    if nblk == 1:
        block(0, 0)
    else:
        def step(i, carry):
            q0 = pl.multiple_of(i * bq, bq)
            ks = pl.multiple_of(jnp.clip(q0 - BAND, 0, n - win), BAND)
            block(q0, ks)
            return carry
        lax.fori_loop(0, nblk, step, 0)


def _dil_attn(q, k, v, *, batch, seq, dil, bq=128):
    n = seq // dil
    win = min(n, bq + 2 * BAND)
    view = lambda a, w: a.reshape(batch, n, dil * w)
    blk = lambda w: pl.BlockSpec((1, n, w), lambda b, c: (b, 0, c))
    o, lse = pl.pallas_call(
        functools.partial(_attn_body, n=n, bq=bq, win=win),
        grid=(batch, dil),
        in_specs=[blk(ATT_OUT)] * 3,
        out_specs=[blk(ATT_OUT), blk(LANE)],
        out_shape=[jax.ShapeDtypeStruct((batch, n, dil * ATT_OUT), BF16),
                   jax.ShapeDtypeStruct((batch, n, dil * LANE), F32)],
        compiler_params=_params("parallel", "parallel"),
        name=f"dil_attn_r{dil}",
    )(view(q, ATT_OUT), view(k, ATT_OUT), view(v, ATT_OUT))
    return o.reshape(batch * seq, ATT_OUT), lse.reshape(batch * seq, LANE)


def _mlstm_body(qk_ref, v_ref, og_ref, gc_ref, gr_ref, cw_ref, cb_ref, gain_ref, o_ref,
                pad_ref, qs_ref, ks_ref, hf_ref, hb_ref, c_ref, n_ref, m_ref, *, seq, chunk, row_block=256):
    halo = 8
    zeros = jnp.zeros((halo, LANE), F32)
    pad_ref[0:halo, :] = zeros
    pad_ref[seq + halo:seq + 2 * halo, :] = zeros
    nrb = seq // row_block
    for c in range(2 * ML_HEADS):
        cs = slice(c * LANE, (c + 1) * LANE)

        def fill(i, carry):
            r0 = pl.multiple_of(i * row_block, row_block)
            pad_ref[pl.ds(r0 + halo, row_block), :] = qk_ref[0, pl.ds(r0, row_block), cs].astype(F32)
            return carry
        lax.fori_loop(0, nrb, fill, 0)

        def conv(i, carry):
            r0 = pl.multiple_of(i * row_block, row_block)
            acc = jnp.zeros((row_block, LANE), F32) + cb_ref[:, cs]
            for j in range(ML_CONV):
                acc = acc + cw_ref[j:j + 1, cs] * pad_ref[pl.ds(r0 + (halo - ML_CONV // 2 + j), row_block), :]
            y = acc * _sigmoid(acc)
            if c < ML_HEADS:
                qs_ref[pl.ds(r0, row_block), cs] = (y * ML_HEAD_DIM ** -0.5).astype(BF16)
            else:
                ks_ref[pl.ds(r0, row_block), (c - ML_HEADS) * LANE:(c - ML_HEADS + 1) * LANE] = y.astype(BF16)
            return carry
        lax.fori_loop(0, nrb, conv, 0)

    c_ref[...] = jnp.zeros_like(c_ref)
    n_ref[...] = jnp.zeros_like(n_ref)
    m_ref[...] = jnp.zeros_like(m_ref)

    nc = seq // chunk
    ri = lax.broadcasted_iota(jnp.int32, (chunk, chunk), 0)
    ci = lax.broadcasted_iota(jnp.int32, (chunk, chunk), 1)
    lower = ri >= ci
    upper = ri <= ci
    lower_f = lower.astype(F32)
    upper_f = upper.astype(F32)

    def step(it, carry):
        for d in range(2):
            cidx = it if d == 0 else nc - 1 - it
            t0 = pl.multiple_of(cidx * chunk, chunk)
            rows = pl.ds(t0, chunk)
            gcol = gc_ref[0, rows, :]
            grow = gr_ref[0, :, rows]
            if d == 0:
                cum_col = _dot_exact(lower_f, _log_sigmoid(gcol))
                cum_row = _dot_exact(_log_sigmoid(grow), upper_f)
                mask = lower
            else:
                cum_col = _dot_exact(upper_f, _log_sigmoid(gcol))
                cum_row = _dot_exact(_log_sigmoid(grow), lower_f)
                mask = upper
            for hd in range(ML_HEADS):
                st = d * ML_HEADS + hd
                i_gate = (2 * d) * ML_HEADS + hd
                f_gate = (2 * d + 1) * ML_HEADS + hd
                hs = slice(hd * LANE, (hd + 1) * LANE)
                li_col = gcol[:, i_gate:i_gate + 1]
                li_row = grow[i_gate:i_gate + 1, :]
                b_col = cum_col[:, f_gate:f_gate + 1]
                b_row = cum_row[f_gate:f_gate + 1, :]
                gtot = b_col[chunk - 1:chunk, :] if d == 0 else b_col[0:1, :]
                m_prev = m_ref[st][0:1, 0:1]
                n_prev = n_ref[st]
                c_prev = c_ref[st]
                q = qs_ref[rows, hs]
                k = ks_ref[rows, hs]
                v = v_ref[0, rows, hs]

                dmat = jnp.where(mask, b_col - b_row + li_row, NEG)
                inter = b_col + m_prev
                mt = jnp.maximum(jnp.max(dmat, axis=-1, keepdims=True), inter)
                wts = jnp.exp(dmat - mt)
                w_inter = jnp.exp(inter - mt)
                sc = _dot_nt(q, k) * wts
                num = _dot(sc.astype(BF16), v) + w_inter * _dot_nt(q, c_prev.astype(BF16))
                den = (jnp.sum(sc, axis=-1, keepdims=True)
                       + w_inter * jnp.sum(q.astype(F32) * n_prev, axis=-1, keepdims=True))
                hout = num / jnp.maximum(jnp.abs(den), jnp.exp(-mt))
                if d == 0:
                    hf_ref[rows, hs] = hout
                else:
                    hb_ref[rows, hs] = hout

                a_col = gtot - b_col + li_col
                a_row = gtot - b_row + li_row
                m_new = jnp.maximum(gtot + m_prev, jnp.max(a_row, axis=-1, keepdims=True))
                decay = jnp.exp(gtot + m_prev - m_new)
                wk = jnp.exp(a_col - m_new)
                vw = (v.astype(F32) * wk).astype(BF16)
                c_ref[st] = decay * c_prev + lax.dot_general(vw, k, TN_DIMS, preferred_element_type=F32)
                n_ref[st] = decay * n_prev + jnp.sum(k.astype(F32) * wk, axis=0, keepdims=True)
                m_ref[st] = jnp.broadcast_to(m_new, m_ref.shape[1:])
        return carry

    lax.fori_loop(0, nc, step, 0)

    def finish(i, carry):
        rows = pl.ds(pl.multiple_of(i * row_block, row_block), row_block)
        for hd in range(ML_HEADS):
            hs = slice(hd * LANE, (hd + 1) * LANE)
            hm = _rms(hf_ref[rows, hs] + hb_ref[rows, hs]) * gain_ref[:, hs]
            o_ref[0, rows, hs] = (hm * og_ref[0, rows, hs].astype(F32)).astype(BF16)
        return carry
    lax.fori_loop(0, nrb, finish, 0)


def _mlstm(mqk, mv, og, gates_col, gates_row, conv_w, conv_b, out_gain, *, batch, seq, chunk=ML_CHUNK):
    halo = 8
    b3 = lambda a: a.reshape(batch, seq, a.shape[-1])
    per_b = lambda w: pl.BlockSpec((1, seq, w), lambda b: (b, 0, 0))
    y = pl.pallas_call(
        functools.partial(_mlstm_body, seq=seq, chunk=chunk),
        grid=(batch,),
        in_specs=[per_b(2 * ML_WIDTH), per_b(ML_WIDTH), per_b(ML_WIDTH), per_b(LANE),
                  pl.BlockSpec((1, ML_GATES, seq), lambda b: (b, 0, 0)),
                  _resident(conv_w.shape), _resident(conv_b.shape), _resident(out_gain.shape)],
        out_specs=per_b(ML_WIDTH),
        out_shape=jax.ShapeDtypeStruct((batch, seq, ML_WIDTH), BF16),
        scratch_shapes=[
            pltpu.VMEM((seq + 2 * halo, LANE), F32),
            pltpu.VMEM((seq, ML_WIDTH), BF16),
            pltpu.VMEM((seq, ML_WIDTH), BF16),
            pltpu.VMEM((seq, ML_WIDTH), F32),
            pltpu.VMEM((seq, ML_WIDTH), F32),
            pltpu.VMEM((2 * ML_HEADS, ML_HEAD_DIM, ML_HEAD_DIM), F32),
            pltpu.VMEM((2 * ML_HEADS, 1, ML_HEAD_DIM), F32),
            pltpu.VMEM((2 * ML_HEADS, 8, LANE), F32),
        ],
        compiler_params=_params("parallel"),
        name="mlstm",
    )(b3(mqk), b3(mv), b3(og), b3(gates_col), gates_row, conv_w, conv_b, out_gain)
    return y.reshape(batch * seq, ML_WIDTH)


def _mem_kv_body(mem_ref, gn_ref, w_ref, gk_ref, ck_ref, cv_ref):
    h = (_rms(mem_ref[0]) * gn_ref[...]).astype(BF16)
    kv = _dot(h, w_ref[...])
    for hd in range(CX_HEADS):
        cs = slice(hd * CX_HEAD_DIM, (hd + 1) * CX_HEAD_DIM)
        ck_ref[0, :, cs] = (_rms(kv[:, cs]) * gk_ref[...]).astype(BF16)
    cv_ref[0] = kv[:, CX_WIDTH:].astype(BF16)


def _mem_kv(mem, gn, w, gk):
    b, mlen, dm = mem.shape
    blk = lambda w_: pl.BlockSpec((1, mlen, w_), lambda i: (i, 0, 0))
    return pl.pallas_call(
        _mem_kv_body,
        grid=(b,),
        in_specs=[blk(dm), _resident((1, dm)), _resident(w.shape), _resident(gk.shape)],
        out_specs=[blk(CX_WIDTH)] * 2,
        out_shape=[jax.ShapeDtypeStruct((b, mlen, CX_WIDTH), BF16)] * 2,
        compiler_params=_params("parallel"),
        name="mem_kv",
    )(mem, gn, w, gk)


def _merge_body(x_ref, o0_ref, o1_ref, o2_ref, l0_ref, l1_ref, l2_ref, yml_ref, cq_ref, g_ref, ck_ref, cv_ref,
                wa_ref, wm_ref, wc_ref, wout_ref, ex_ref, out_ref):
    dm = x_ref.shape[1]
    lses = (l0_ref[...], l1_ref[...], l2_ref[...])
    top = jnp.maximum(jnp.maximum(lses[0], lses[1]), lses[2])
    es = [jnp.exp(l - top) for l in lses]
    inv = 1.0 / (es[0] + es[1] + es[2])
    ex = ex_ref[...]
    y_att = jnp.zeros((x_ref.shape[0], ATT_OUT), F32)
    for e, o_ref in zip(es, (o0_ref, o1_ref, o2_ref)):
        alpha = e * inv
        hi = alpha.astype(BF16)
        lo = (alpha - hi.astype(F32)).astype(BF16)
        y_att = y_att + (_dot(hi, ex) + _dot(lo, ex)) * o_ref[...].astype(F32)

    y_cx = []
    for hd in range(CX_HEADS):
        cs = slice(hd * CX_HEAD_DIM, (hd + 1) * CX_HEAD_DIM)
        s = _dot_nt(cq_ref[:, cs], ck_ref[0, :, cs])
        p = jnp.exp(s - jnp.max(s, axis=-1, keepdims=True))
        l = jnp.sum(p, axis=-1, keepdims=True)
        y_cx.append(_dot(p.astype(BF16), cv_ref[0, :, cs]) / l)
    y_cx = jnp.concatenate(y_cx, axis=1)

    merged = (g_ref[:, 0:dm].astype(F32) * _dot(y_att.astype(BF16), wa_ref[...])
              + g_ref[:, dm:2 * dm].astype(F32) * _dot(yml_ref[...], wm_ref[...])
              + g_ref[:, 2 * dm:3 * dm].astype(F32) * _dot(y_cx.astype(BF16), wc_ref[...]))
    out_ref[...] = x_ref[...] + _dot(merged.astype(BF16), wout_ref[...])


def _merge(x, o_att, lse_att, y_ml, cq, gates, ck, cv, wa, wm, wc, wout, expand, *, seq, tm=256):
    t, dm = x.shape
    mlen = ck.shape[1]
    nseq = seq // tm
    tok = lambda w: pl.BlockSpec((tm, w), lambda i: (i, 0))
    mem = pl.BlockSpec((1, mlen, CX_WIDTH), lambda i: (i // nseq, 0, 0))
    return pl.pallas_call(
        _merge_body,
        grid=(t // tm,),
        in_specs=[tok(dm), tok(ATT_OUT), tok(ATT_OUT), tok(ATT_OUT), tok(LANE), tok(LANE), tok(LANE),
                  tok(ML_WIDTH), tok(CX_WIDTH), tok(N_BRANCH * dm), mem, mem,
                  _resident(wa.shape), _resident(wm.shape), _resident(wc.shape), _resident(wout.shape),
                  _resident(expand.shape)],
        out_specs=tok(dm),
        out_shape=jax.ShapeDtypeStruct((t, dm), F32),
        compiler_params=_params("parallel"),
        name="merge",
    )(x, *o_att, *lse_att, y_ml, cq, gates, ck, cv, wa, wm, wc, wout, expand)


def _rope_tables(seq):
    d = ATT_HEAD_DIM
    inv = ROPE_THETA ** (-jnp.arange(0, d, 2, dtype=F32) / d)
    ang = jnp.arange(seq, dtype=F32)[:, None] * inv[None, :]
    cos, sin = jnp.cos(ang), jnp.sin(ang)
    cos_t = jnp.tile(jnp.concatenate([cos, cos], axis=-1), (1, ATT_SLOTS))
    sin_t = jnp.tile(jnp.concatenate([-sin, sin], axis=-1), (1, ATT_SLOTS))
    return cos_t, sin_t


def _layer(x, mem, p):
    batch, seq, dm = x.shape
    xt = x.reshape(batch * seq, dm)
    row = lambda a: a.reshape(1, -1).astype(F32)
    bf = lambda a: a.astype(BF16)

    x1 = _ffn(xt, row(p["norm_ffn1"]), bf(p["w_ffn1_in"]), bf(p["w_ffn1_out"]), row(p["norm_final"]),
              final_norm=False)

    w_in = bf(p["w_in"])
    sizes = (ATT_QKV, ATT_QKV, ATT_QKV, ML_WIDTH, ML_WIDTH, ML_WIDTH, ML_WIDTH, ML_GATES, CX_WIDTH, N_BRANCH * dm)
    offs = [0]
    for s in sizes:
        offs.append(offs[-1] + s)
    col = lambda i, j=None: w_in[:, offs[i]:offs[(i if j is None else j) + 1]]
    gn = row(p["norm_mix"])

    cos_t, sin_t = _rope_tables(seq)
    idx = jnp.arange(MXU_DIM) // ATT_HEAD_DIM
    e64 = jnp.where(idx[:, None] == idx[None, :], 1.0 / ATT_HEAD_DIM, 0.0).astype(BF16)
    tile8 = lambda a: jnp.tile(row(a), (1, ATT_SLOTS))
    qkv = _qkv_proj(x1, gn, col(0), col(1), col(2), tile8(p["att_q_gain"]), tile8(p["att_k_gain"]),
                    cos_t, sin_t, e64, seq=seq)

    w_if = jnp.pad(col(7), ((0, 0), (0, LANE - ML_GATES)))
    b_if = jnp.pad(row(p["ml_gate_b"]), ((0, 0), (0, LANE - ML_GATES)))
    mqk, mv, og, mif, cq, gates = _misc_proj(
        x1, gn, col(3, 4), col(5), col(6), w_if, b_if, col(8), row(p["cx_q_gain"]), col(9),
        row(p["mix_gate_b"]))

    o_att, lse_att = [], []
    for g, (_, dil) in enumerate(DIL_GROUPS):
        o, lse = _dil_attn(qkv[g], qkv[3 + g], qkv[6 + g], batch=batch, seq=seq, dil=dil)
        o_att.append(o)
        lse_att.append(lse)

    gates_row = jnp.swapaxes(mif.reshape(batch, seq, LANE)[:, :, :ML_GATES], 1, 2)
    y_ml = _mlstm(mqk, mv, og, mif, gates_row, p["ml_conv_w"].astype(F32), row(p["ml_conv_b"]),
                  row(p["ml_out_gain"]), batch=batch, seq=seq)

    ck, cv = _mem_kv(mem, row(p["norm_mem"]), bf(p["w_mem_kv"]), row(p["cx_k_gain"]))

    lanes = jnp.arange(ATT_OUT) // ATT_HEAD_DIM * LSE_LANES_PER_HEAD
    expand = (jnp.arange(LANE)[:, None] == lanes[None, :]).astype(BF16)
    x2 = _merge(x1, o_att, lse_att, y_ml, cq, gates, ck, cv, bf(p["w_br_att"]), bf(p["w_br_ml"]),
                bf(p["w_br_cx"]), bf(p["w_out"]), expand, seq=seq)

    x3 = _ffn(x2, row(p["norm_ffn2"]), bf(p["w_ffn2_in"]), bf(p["w_ffn2_out"]), row(p["norm_final"]),
              final_norm=True)
    return x3.reshape(batch, seq, dm)


def kernel(x, mem, norm_ffn1, w_ffn1_in, w_ffn1_out, norm_mix, norm_mem, w_in, att_q_gain, att_k_gain,
           ml_conv_w, ml_conv_b, ml_gate_b, ml_out_gain, cx_q_gain, cx_k_gain, w_mem_kv, mix_gate_b,
           w_br_att, w_br_ml, w_br_cx, w_out, norm_ffn2, w_ffn2_in, w_ffn2_out, norm_final):
    params = dict(norm_ffn1=norm_ffn1, w_ffn1_in=w_ffn1_in, w_ffn1_out=w_ffn1_out, norm_mix=norm_mix,
                  norm_mem=norm_mem, w_in=w_in, att_q_gain=att_q_gain, att_k_gain=att_k_gain,
                  ml_conv_w=ml_conv_w, ml_conv_b=ml_conv_b, ml_gate_b=ml_gate_b, ml_out_gain=ml_out_gain,
                  cx_q_gain=cx_q_gain, cx_k_gain=cx_k_gain, w_mem_kv=w_mem_kv, mix_gate_b=mix_gate_b,
                  w_br_att=w_br_att, w_br_ml=w_br_ml, w_br_cx=w_br_cx, w_out=w_out, norm_ffn2=norm_ffn2,
                  w_ffn2_in=w_ffn2_in, w_ffn2_out=w_ffn2_out, norm_final=norm_final)
    depth = norm_ffn1.shape[0]
    for layer in range(depth):
        x = _layer(x, mem, {k: v[layer] for k, v in params.items()})
    return x
```

```python
import functools

import jax
import jax.numpy as jnp
from jax import lax
from jax.experimental import pallas as pl
from jax.experimental.pallas import tpu as pltpu

F32 = jnp.float32
BF16 = jnp.bfloat16

EPS = 1e-6
ROPE_THETA = 10000.0
NEG = -1e30

ATT_HEAD_DIM = 64
ATT_SLOTS = 8
DIL_GROUPS = ((128, 1), (512, 4), (2048, 16))
ATT_OUT = ATT_SLOTS * ATT_HEAD_DIM
ATT_QKV = len(DIL_GROUPS) * ATT_OUT
ML_HEADS = 4
ML_HEAD_DIM = 128
ML_WIDTH = ML_HEADS * ML_HEAD_DIM
ML_GATES = 4 * ML_HEADS
ML_CHUNK = 128
ML_CONV = 5
CX_HEADS = 4
CX_HEAD_DIM = 128
CX_WIDTH = CX_HEADS * CX_HEAD_DIM
N_BRANCH = 3

LANE = 128
MXU_DIM = 256
VMEM_LIMIT = 56 * 1024 * 1024

LSE_LANES_PER_HEAD = LANE // ATT_SLOTS
QKV_TILE = 256
BAND = 64
assert all((w // 2) // r == BAND for w, r in DIL_GROUPS)

NT_DIMS = (((1,), (1,)), ((), ()))
TN_DIMS = (((0,), (0,)), ((), ()))


def _dot(a, b):
    return jnp.dot(a, b, preferred_element_type=F32)


def _dot_nt(a, b):
    return lax.dot_general(a, b, NT_DIMS, preferred_element_type=F32)


def _dot_exact(a, b):
    return jnp.dot(a, b, preferred_element_type=F32, precision=lax.Precision.HIGHEST)


def _rms(x):
    return x * lax.rsqrt(jnp.mean(x * x, axis=-1, keepdims=True) + EPS)


def _sigmoid(x):
    return 1.0 / (1.0 + jnp.exp(-x))


def _log_sigmoid(x):
    return jnp.minimum(x, 0.0) - jnp.log(1.0 + jnp.exp(-jnp.abs(x)))


def _params(*sem):
    return pltpu.CompilerParams(dimension_semantics=sem, vmem_limit_bytes=VMEM_LIMIT)


def _resident(shape):
    return pl.BlockSpec(shape, lambda *_: (0,) * len(shape), pipeline_mode=pl.Buffered(1))


def _ffn_body(x_ref, g_ref, wg_ref, wu_ref, wo_ref, gf_ref, o_ref, h_ref, acc_ref, *, final_norm):
    j = pl.program_id(1)

    @pl.when(j == 0)
    def _():
        h_ref[...] = (_rms(x_ref[...]) * g_ref[...]).astype(BF16)
        acc_ref[...] = jnp.zeros_like(acc_ref)

    h = h_ref[...]
    gate = _dot(h, wg_ref[...])
    up = _dot(h, wu_ref[...])
    a = (gate * _sigmoid(gate) * up).astype(BF16)
    acc_ref[...] += _dot(a, wo_ref[...])

    @pl.when(j == pl.num_programs(1) - 1)
    def _():
        y = x_ref[...] + 0.5 * acc_ref[...]
        if final_norm:
            y = _rms(y) * gf_ref[...]
        o_ref[...] = y


def _ffn(x, gain, w_in, w_out, final_gain, *, final_norm, tm=512, tf=MXU_DIM):
    t, dm = x.shape
    d_ff = w_out.shape[0]
    nj = d_ff // tf
    return pl.pallas_call(
        functools.partial(_ffn_body, final_norm=final_norm),
        grid=(t // tm, nj),
        in_specs=[
            pl.BlockSpec((tm, dm), lambda i, j: (i, 0)),
            pl.BlockSpec((1, dm), lambda i, j: (0, 0)),
            pl.BlockSpec((dm, tf), lambda i, j: (0, j)),
            pl.BlockSpec((dm, tf), lambda i, j: (0, j + nj)),
            pl.BlockSpec((tf, dm), lambda i, j: (j, 0)),
            pl.BlockSpec((1, dm), lambda i, j: (0, 0)),
        ],
        out_specs=pl.BlockSpec((tm, dm), lambda i, j: (i, 0)),
        out_shape=jax.ShapeDtypeStruct((t, dm), F32),
        scratch_shapes=[pltpu.VMEM((tm, dm), BF16), pltpu.VMEM((tm, dm), F32)],
        compiler_params=_params("parallel", "arbitrary"),
        name="ffn_final" if final_norm else "ffn",
    )(x, gain, w_in, w_in, w_out, final_gain)


def _qkv_body(x_ref, gn_ref, wq_ref, wk_ref, wv_ref, gq_ref, gk_ref, cos0_ref, sin0_ref, cos1_ref, sin1_ref,
              cos2_ref, sin2_ref, e_ref, *rest):
    outs, (hs_ref, hp_ref) = rest[:9], rest[9:]
    q_outs, k_outs, v_outs = outs[0:3], outs[3:6], outs[6:9]
    tables = ((cos0_ref, sin0_ref), (cos1_ref, sin1_ref), (cos2_ref, sin2_ref))
    tm, dm = x_ref.shape
    hn = _rms(x_ref[...]) * gn_ref[...]
    for sl in range(dm // LANE):
        hs_ref[sl] = hn[:, sl * LANE:(sl + 1) * LANE]
    lane = lax.broadcasted_iota(jnp.int32, (tm, ATT_OUT), 1)
    first_half = (lane & (ATT_HEAD_DIM - 1)) < ATT_HEAD_DIM // 2
    e = e_ref[...]
    for g, (_, dil) in enumerate(DIL_GROUPS):
        m = tm // dil
        if dil == 1:
            h = hn.astype(BF16)
        else:
            for c in range(dil):
                for sl in range(dm // LANE):
                    hp_ref[c * m:(c + 1) * m, sl * LANE:(sl + 1) * LANE] = (
                        hs_ref[sl, pl.ds(c, m, stride=dil), :].astype(BF16))
            h = hp_ref[...]
        cos = tables[g][0][...]
        sin = tables[g][1][...]
        cs = slice(g * ATT_OUT, (g + 1) * ATT_OUT)

        def emit(out, val):
            for c in range(dil):
                out[0, c] = val[c * m:(c + 1) * m, :].astype(BF16)

        for w_ref, gain_ref, scale, out in ((wq_ref, gq_ref, ATT_HEAD_DIM ** -0.5, q_outs[g]),
                                            (wk_ref, gk_ref, 1.0, k_outs[g])):
            t = _dot(h, w_ref[:, cs])
            sq = (t * t).astype(BF16)
            ms = jnp.concatenate([_dot(sq[:, c:c + MXU_DIM], e) for c in range(0, ATT_OUT, MXU_DIM)], axis=1)
            tn = t * lax.rsqrt(ms + EPS) * gain_ref[...]
            partner = jnp.where(first_half,
                                pltpu.roll(tn, ATT_OUT - ATT_HEAD_DIM // 2, 1),
                                pltpu.roll(tn, ATT_HEAD_DIM // 2, 1))
            emit(out, (tn * cos + partner * sin) * scale)
        emit(v_outs[g], _dot(h, wv_ref[:, cs]))


def _qkv_proj(x, gn, wq, wk, wv, gq, gk, tables, e64, *, batch, seq, tm=256):
    t, dm = x.shape
    nseq = seq // tm
    pos = pl.BlockSpec((tm, ATT_OUT), lambda i: (i % nseq, 0))
    out_specs, out_shape = [], []
    for _ in range(3):
        for _, dil in DIL_GROUPS:
            out_specs.append(pl.BlockSpec((1, dil, tm // dil, ATT_OUT), lambda i: (i // nseq, 0, i % nseq, 0)))
            out_shape.append(jax.ShapeDtypeStruct((batch, dil, seq // dil, ATT_OUT), BF16))
    return pl.pallas_call(
        _qkv_body,
        grid=(t // tm,),
        in_specs=[pl.BlockSpec((tm, dm), lambda i: (i, 0)), _resident((1, dm)), _resident((dm, ATT_QKV)),
                  _resident((dm, ATT_QKV)), _resident((dm, ATT_QKV)), _resident((1, ATT_OUT)),
                  _resident((1, ATT_OUT))] + [pos] * 6 + [_resident((MXU_DIM, MXU_DIM))],
        out_specs=out_specs,
        out_shape=out_shape,
        scratch_shapes=[pltpu.VMEM((dm // LANE, tm, LANE), F32),
                        pltpu.VMEM((tm, dm), BF16)],
        compiler_params=_params("parallel"),
        name="qkv_proj",
    )(x, gn, wq, wk, wv, gq, gk, *tables, e64)


def _misc_body(x_ref, gn_ref, wqk_ref, wv_ref, wo_ref, wif_ref, bif_ref, wcq_ref, gcq_ref, wg_ref, bg_ref,
               mqk_ref, mv_ref, og_ref, mif_ref, cq_ref, g_ref):
    h = (_rms(x_ref[...]) * gn_ref[...]).astype(BF16)
    mqk_ref[...] = _dot(h, wqk_ref[...]).astype(BF16)
    mv_ref[...] = _dot(h, wv_ref[...]).astype(BF16)
    og_ref[...] = _sigmoid(_dot(h, wo_ref[...])).astype(BF16)
    mif_ref[...] = _dot(h, wif_ref[...]) + bif_ref[...]
    t = _dot(h, wcq_ref[...])
    for hd in range(CX_HEADS):
        cs = slice(hd * CX_HEAD_DIM, (hd + 1) * CX_HEAD_DIM)
        cq_ref[:, cs] = (_rms(t[:, cs]) * gcq_ref[...] * CX_HEAD_DIM ** -0.5).astype(BF16)
    dm = x_ref.shape[1]
    for br in range(N_BRANCH):
        cs = slice(br * dm, (br + 1) * dm)
        g_ref[:, cs] = _sigmoid(_dot(h, wg_ref[:, cs]) + bg_ref[:, cs]).astype(BF16)


def _misc_proj(x, gn, wqk, wv, wo, wif, bif, wcq, gcq, wg, bg, *, tm=256):
    t, dm = x.shape
    tok = lambda w: pl.BlockSpec((tm, w), lambda i: (i, 0))
    widths = (2 * ML_WIDTH, ML_WIDTH, ML_WIDTH, LANE, CX_WIDTH, N_BRANCH * dm)
    dtypes = (BF16, BF16, BF16, F32, BF16, BF16)
    return pl.pallas_call(
        _misc_body,
        grid=(t // tm,),
        in_specs=[tok(dm), _resident((1, dm)), _resident(wqk.shape), _resident(wv.shape), _resident(wo.shape),
                  _resident(wif.shape), _resident(bif.shape), _resident(wcq.shape), _resident(gcq.shape),
                  _resident(wg.shape), _resident(bg.shape)],
        out_specs=[tok(w) for w in widths],
        out_shape=[jax.ShapeDtypeStruct((t, w), d) for w, d in zip(widths, dtypes)],
        compiler_params=_params("parallel"),
        name="misc_proj",
    )(x, gn, wqk, wv, wo, wif, bif, wcq, gcq, wg, bg)


def _attn_body(q_ref, k_ref, v_ref, o_ref, lse_ref, *, dil, n, bq, win):
    lane = lax.broadcasted_iota(jnp.int32, (bq, LANE), 1)
    low_head = lane < ATT_HEAD_DIM
    lse_group = lane // LSE_LANES_PER_HEAD
    qi = lax.broadcasted_iota(jnp.int32, (bq, win), 0)
    ki = lax.broadcasted_iota(jnp.int32, (bq, win), 1)

    def block(c, q0, ks):
        valid = jnp.abs((q0 + qi) - (ks + ki)) <= BAND
        lse_row = jnp.zeros((bq, LANE), F32)
        for pair in range(ATT_SLOTS // 2):
            cs = slice(pair * LANE, (pair + 1) * LANE)
            qp = q_ref[0, c, pl.ds(q0, bq), cs]
            kp = k_ref[0, c, pl.ds(ks, win), cs]
            vp = v_ref[0, c, pl.ds(ks, win), cs]
            halves = []
            for half in range(2):
                qm = jnp.where(low_head if half == 0 else ~low_head, qp, jnp.zeros_like(qp))
                s = jnp.where(valid, _dot_nt(qm, kp), NEG)
                m = jnp.max(s, axis=-1, keepdims=True)
                p = jnp.exp(s - m)
                l = jnp.sum(p, axis=-1, keepdims=True)
                halves.append(_dot(p.astype(BF16), vp) / l)
                lse_row = jnp.where(lse_group == 2 * pair + half, m + jnp.log(l), lse_row)
            o_ref[0, c, pl.ds(q0, bq), cs] = jnp.where(low_head, halves[0], halves[1]).astype(BF16)
        lse_ref[0, c, pl.ds(q0, bq), :] = lse_row

    nblk = n // bq

    def step(i, carry):
        c = i // nblk
        q0 = pl.multiple_of((i % nblk) * bq, bq)
        ks = pl.multiple_of(jnp.clip(q0 - BAND, 0, n - win), BAND)
        block(c, q0, ks)
        return carry
    lax.fori_loop(0, dil * nblk, step, 0)


def _dil_attn(q, k, v, *, dil, bq=128):
    batch, _, n, _ = q.shape
    win = min(n, bq + 2 * BAND)
    blk = lambda w: pl.BlockSpec((1, dil, n, w), lambda b: (b, 0, 0, 0))
    return pl.pallas_call(
        functools.partial(_attn_body, dil=dil, n=n, bq=bq, win=win),
        grid=(batch,),
        in_specs=[blk(ATT_OUT)] * 3,
        out_specs=[blk(ATT_OUT), blk(LANE)],
        out_shape=[jax.ShapeDtypeStruct((batch, dil, n, ATT_OUT), BF16),
                   jax.ShapeDtypeStruct((batch, dil, n, LANE), F32)],
        compiler_params=_params("parallel"),
        name=f"dil_attn_r{dil}",
    )(q, k, v)


def _mlstm_body(qk_ref, v_ref, og_ref, gc_ref, gr_ref, cw_ref, cb_ref, gain_ref, o_ref,
                pad_ref, qs_ref, ks_ref, hf_ref, hb_ref, c_ref, n_ref, m_ref, *, seq, chunk, row_block=256):
    halo = 8
    zeros = jnp.zeros((halo, LANE), F32)
    pad_ref[0:halo, :] = zeros
    pad_ref[seq + halo:seq + 2 * halo, :] = zeros
    nrb = seq // row_block
    for c in range(2 * ML_HEADS):
        cs = slice(c * LANE, (c + 1) * LANE)

        def fill(i, carry):
            r0 = pl.multiple_of(i * row_block, row_block)
            pad_ref[pl.ds(r0 + halo, row_block), :] = qk_ref[0, pl.ds(r0, row_block), cs].astype(F32)
            return carry
        lax.fori_loop(0, nrb, fill, 0)

        def conv(i, carry):
            r0 = pl.multiple_of(i * row_block, row_block)
            acc = jnp.zeros((row_block, LANE), F32) + cb_ref[:, cs]
            for j in range(ML_CONV):
                acc = acc + cw_ref[j:j + 1, cs] * pad_ref[pl.ds(r0 + (halo - ML_CONV // 2 + j), row_block), :]
            y = acc * _sigmoid(acc)
            if c < ML_HEADS:
                qs_ref[pl.ds(r0, row_block), cs] = (y * ML_HEAD_DIM ** -0.5).astype(BF16)
            else:
                ks_ref[pl.ds(r0, row_block), (c - ML_HEADS) * LANE:(c - ML_HEADS + 1) * LANE] = y.astype(BF16)
            return carry
        lax.fori_loop(0, nrb, conv, 0)

    c_ref[...] = jnp.zeros_like(c_ref)
    n_ref[...] = jnp.zeros_like(n_ref)
    m_ref[...] = jnp.zeros_like(m_ref)

    nc = seq // chunk
    ri = lax.broadcasted_iota(jnp.int32, (chunk, chunk), 0)
    ci = lax.broadcasted_iota(jnp.int32, (chunk, chunk), 1)
    lower = ri >= ci
    upper = ri <= ci
    lower_f = lower.astype(F32)
    upper_f = upper.astype(F32)

    def step(it, carry):
        for d in range(2):
            cidx = it if d == 0 else nc - 1 - it
            t0 = pl.multiple_of(cidx * chunk, chunk)
            rows = pl.ds(t0, chunk)
            gcol = gc_ref[0, rows, :]
            grow = gr_ref[0, :, rows]
            if d == 0:
                cum_col = _dot_exact(lower_f, _log_sigmoid(gcol))
                cum_row = _dot_exact(_log_sigmoid(grow), upper_f)
                mask = lower
            else:
                cum_col = _dot_exact(upper_f, _log_sigmoid(gcol))
                cum_row = _dot_exact(_log_sigmoid(grow), lower_f)
                mask = upper
            for hd in range(ML_HEADS):
                st = d * ML_HEADS + hd
                i_gate = (2 * d) * ML_HEADS + hd
                f_gate = (2 * d + 1) * ML_HEADS + hd
                hs = slice(hd * LANE, (hd + 1) * LANE)
                li_col = gcol[:, i_gate:i_gate + 1]
                li_row = grow[i_gate:i_gate + 1, :]
                b_col = cum_col[:, f_gate:f_gate + 1]
                b_row = cum_row[f_gate:f_gate + 1, :]
                gtot = b_col[chunk - 1:chunk, :] if d == 0 else b_col[0:1, :]
                m_prev = m_ref[st][0:1, 0:1]
                n_prev = n_ref[st]
                c_prev = c_ref[st]
                q = qs_ref[rows, hs]
                k = ks_ref[rows, hs]
                v = v_ref[0, rows, hs]

                dmat = jnp.where(mask, b_col - b_row + li_row, NEG)
                inter = b_col + m_prev
                mt = jnp.maximum(jnp.max(dmat, axis=-1, keepdims=True), inter)
                wts = jnp.exp(dmat - mt)
                w_inter = jnp.exp(inter - mt)
                sc = _dot_nt(q, k) * wts
                num = _dot(sc.astype(BF16), v) + w_inter * _dot_nt(q, c_prev.astype(BF16))
                den = (jnp.sum(sc, axis=-1, keepdims=True)
                       + w_inter * jnp.sum(q.astype(F32) * n_prev, axis=-1, keepdims=True))
                hout = num / jnp.maximum(jnp.abs(den), jnp.exp(-mt))
                if d == 0:
                    hf_ref[rows, hs] = hout
                else:
                    hb_ref[rows, hs] = hout

                a_col = gtot - b_col + li_col
                a_row = gtot - b_row + li_row
                m_new = jnp.maximum(gtot + m_prev, jnp.max(a_row, axis=-1, keepdims=True))
                decay = jnp.exp(gtot + m_prev - m_new)
                wk = jnp.exp(a_col - m_new)
                vw = (v.astype(F32) * wk).astype(BF16)
                c_ref[st] = decay * c_prev + lax.dot_general(vw, k, TN_DIMS, preferred_element_type=F32)
                n_ref[st] = decay * n_prev + jnp.sum(k.astype(F32) * wk, axis=0, keepdims=True)
                m_ref[st] = jnp.broadcast_to(m_new, m_ref.shape[1:])
        return carry

    lax.fori_loop(0, nc, step, 0)

    def finish(i, carry):
        rows = pl.ds(pl.multiple_of(i * row_block, row_block), row_block)
        for hd in range(ML_HEADS):
            hs = slice(hd * LANE, (hd + 1) * LANE)
            hm = _rms(hf_ref[rows, hs] + hb_ref[rows, hs]) * gain_ref[:, hs]
            o_ref[0, rows, hs] = (hm * og_ref[0, rows, hs].astype(F32)).astype(BF16)
        return carry
    lax.fori_loop(0, nrb, finish, 0)


def _mlstm(mqk, mv, og, gates_col, gates_row, conv_w, conv_b, out_gain, *, batch, seq, chunk=ML_CHUNK):
    halo = 8
    b3 = lambda a: a.reshape(batch, seq, a.shape[-1])
    per_b = lambda w: pl.BlockSpec((1, seq, w), lambda b: (b, 0, 0))
    y = pl.pallas_call(
        functools.partial(_mlstm_body, seq=seq, chunk=chunk),
        grid=(batch,),
        in_specs=[per_b(2 * ML_WIDTH), per_b(ML_WIDTH), per_b(ML_WIDTH), per_b(LANE),
                  pl.BlockSpec((1, ML_GATES, seq), lambda b: (b, 0, 0)),
                  _resident(conv_w.shape), _resident(conv_b.shape), _resident(out_gain.shape)],
        out_specs=per_b(ML_WIDTH),
        out_shape=jax.ShapeDtypeStruct((batch, seq, ML_WIDTH), BF16),
        scratch_shapes=[
            pltpu.VMEM((seq + 2 * halo, LANE), F32),
            pltpu.VMEM((seq, ML_WIDTH), BF16),
            pltpu.VMEM((seq, ML_WIDTH), BF16),
            pltpu.VMEM((seq, ML_WIDTH), F32),
            pltpu.VMEM((seq, ML_WIDTH), F32),
            pltpu.VMEM((2 * ML_HEADS, ML_HEAD_DIM, ML_HEAD_DIM), F32),
            pltpu.VMEM((2 * ML_HEADS, 1, ML_HEAD_DIM), F32),
            pltpu.VMEM((2 * ML_HEADS, 8, LANE), F32),
        ],
        compiler_params=_params("parallel"),
        name="mlstm",
    )(b3(mqk), b3(mv), b3(og), b3(gates_col), gates_row, conv_w, conv_b, out_gain)
    return y.reshape(batch * seq, ML_WIDTH)


def _mem_kv_body(mem_ref, gn_ref, w_ref, gk_ref, ck_ref, cv_ref):
    h = (_rms(mem_ref[0]) * gn_ref[...]).astype(BF16)
    kv = _dot(h, w_ref[...])
    for hd in range(CX_HEADS):
        cs = slice(hd * CX_HEAD_DIM, (hd + 1) * CX_HEAD_DIM)
        ck_ref[0, :, cs] = (_rms(kv[:, cs]) * gk_ref[...]).astype(BF16)
    cv_ref[0] = kv[:, CX_WIDTH:].astype(BF16)


def _mem_kv(mem, gn, w, gk):
    b, mlen, dm = mem.shape
    blk = lambda w_: pl.BlockSpec((1, mlen, w_), lambda i: (i, 0, 0))
    return pl.pallas_call(
        _mem_kv_body,
        grid=(b,),
        in_specs=[blk(dm), _resident((1, dm)), _resident(w.shape), _resident(gk.shape)],
        out_specs=[blk(CX_WIDTH)] * 2,
        out_shape=[jax.ShapeDtypeStruct((b, mlen, CX_WIDTH), BF16)] * 2,
        compiler_params=_params("parallel"),
        name="mem_kv",
    )(mem, gn, w, gk)


def _merge_body(x_ref, o0_ref, o1_ref, o2_ref, l0_ref, l1_ref, l2_ref, yml_ref, cq_ref, g_ref, ck_ref, cv_ref,
                wa_ref, wm_ref, wc_ref, wout_ref, ex_ref, out_ref, nat_ref):
    tm, dm = x_ref.shape
    nslab = ATT_OUT // LANE

    def natural(ref, dil, width):
        if dil == 1:
            return ref[0, 0].astype(F32)
        m = tm // dil
        for c in range(dil):
            for sl in range(width // LANE):
                nat_ref[sl, pl.ds(c, m, stride=dil), :] = ref[0, c, :, sl * LANE:(sl + 1) * LANE].astype(F32)
        return jnp.concatenate([nat_ref[sl] for sl in range(width // LANE)], axis=1)

    dils = [dil for _, dil in DIL_GROUPS]
    lses = [natural(ref, dil, LANE) for ref, dil in zip((l0_ref, l1_ref, l2_ref), dils)]
    top = jnp.maximum(jnp.maximum(lses[0], lses[1]), lses[2])
    es = [jnp.exp(l - top) for l in lses]
    inv = 1.0 / (es[0] + es[1] + es[2])
    ex = ex_ref[...]
    y_att = jnp.zeros((tm, ATT_OUT), F32)
    for e, o_ref, dil in zip(es, (o0_ref, o1_ref, o2_ref), dils):
        alpha = e * inv
        hi = alpha.astype(BF16)
        lo = (alpha - hi.astype(F32)).astype(BF16)
        y_att = y_att + (_dot(hi, ex) + _dot(lo, ex)) * natural(o_ref, dil, nslab * LANE)

    y_cx = []
    for hd in range(CX_HEADS):
        cs = slice(hd * CX_HEAD_DIM, (hd + 1) * CX_HEAD_DIM)
        s = _dot_nt(cq_ref[:, cs], ck_ref[0, :, cs])
        p = jnp.exp(s - jnp.max(s, axis=-1, keepdims=True))
        l = jnp.sum(p, axis=-1, keepdims=True)
        y_cx.append(_dot(p.astype(BF16), cv_ref[0, :, cs]) / l)
    y_cx = jnp.concatenate(y_cx, axis=1)

    merged = (g_ref[:, 0:dm].astype(F32) * _dot(y_att.astype(BF16), wa_ref[...])
              + g_ref[:, dm:2 * dm].astype(F32) * _dot(yml_ref[...], wm_ref[...])
              + g_ref[:, 2 * dm:3 * dm].astype(F32) * _dot(y_cx.astype(BF16), wc_ref[...]))
    out_ref[...] = x_ref[...] + _dot(merged.astype(BF16), wout_ref[...])


def _merge(x, o_att, lse_att, y_ml, cq, gates, ck, cv, wa, wm, wc, wout, expand, *, seq, tm=256):
    t, dm = x.shape
    mlen = ck.shape[1]
    nseq = seq // tm
    tok = lambda w: pl.BlockSpec((tm, w), lambda i: (i, 0))
    mem = pl.BlockSpec((1, mlen, CX_WIDTH), lambda i: (i // nseq, 0, 0))
    res = lambda dil, w: pl.BlockSpec((1, dil, tm // dil, w), lambda i: (i // nseq, 0, i % nseq, 0))
    dils = [dil for _, dil in DIL_GROUPS]
    return pl.pallas_call(
        _merge_body,
        grid=(t // tm,),
        in_specs=[tok(dm)] + [res(dil, ATT_OUT) for dil in dils] + [res(dil, LANE) for dil in dils]
                 + [tok(ML_WIDTH), tok(CX_WIDTH), tok(N_BRANCH * dm), mem, mem,
                    _resident(wa.shape), _resident(wm.shape), _resident(wc.shape), _resident(wout.shape),
                    _resident(expand.shape)],
        out_specs=tok(dm),
        out_shape=jax.ShapeDtypeStruct((t, dm), F32),
        scratch_shapes=[pltpu.VMEM((ATT_OUT // LANE, tm, LANE), F32)],
        compiler_params=_params("parallel"),
        name="merge",
    )(x, *o_att, *lse_att, y_ml, cq, gates, ck, cv, wa, wm, wc, wout, expand)


def _rope_tables(seq, tm):
    d = ATT_HEAD_DIM
    inv = ROPE_THETA ** (-jnp.arange(0, d, 2, dtype=F32) / d)
    ang = jnp.arange(seq, dtype=F32)[:, None] * inv[None, :]
    cos, sin = jnp.cos(ang), jnp.sin(ang)
    cos_t = jnp.tile(jnp.concatenate([cos, cos], axis=-1), (1, ATT_SLOTS))
    sin_t = jnp.tile(jnp.concatenate([-sin, sin], axis=-1), (1, ATT_SLOTS))
    tables = []
    for _, dil in DIL_GROUPS:
        perm = lambda a: a.reshape(seq // tm, tm // dil, dil, -1).swapaxes(1, 2).reshape(seq, -1)
        tables += [perm(cos_t), perm(sin_t)]
    return tables


def _layer(x, mem, p):
    batch, seq, dm = x.shape
    xt = x.reshape(batch * seq, dm)
    row = lambda a: a.reshape(1, -1).astype(F32)
    bf = lambda a: a.astype(BF16)

    x1 = _ffn(xt, row(p["norm_ffn1"]), bf(p["w_ffn1_in"]), bf(p["w_ffn1_out"]), row(p["norm_final"]),
              final_norm=False)

    w_in = bf(p["w_in"])
    sizes = (ATT_QKV, ATT_QKV, ATT_QKV, ML_WIDTH, ML_WIDTH, ML_WIDTH, ML_WIDTH, ML_GATES, CX_WIDTH, N_BRANCH * dm)
    offs = [0]
    for s in sizes:
        offs.append(offs[-1] + s)
    col = lambda i, j=None: w_in[:, offs[i]:offs[(i if j is None else j) + 1]]
    gn = row(p["norm_mix"])

    idx = jnp.arange(MXU_DIM) // ATT_HEAD_DIM
    e64 = jnp.where(idx[:, None] == idx[None, :], 1.0 / ATT_HEAD_DIM, 0.0).astype(BF16)
    tile8 = lambda a: jnp.tile(row(a), (1, ATT_SLOTS))
    qkv = _qkv_proj(x1, gn, col(0), col(1), col(2), tile8(p["att_q_gain"]), tile8(p["att_k_gain"]),
                    _rope_tables(seq, QKV_TILE), e64, batch=batch, seq=seq, tm=QKV_TILE)

    w_if = jnp.pad(col(7), ((0, 0), (0, LANE - ML_GATES)))
    b_if = jnp.pad(row(p["ml_gate_b"]), ((0, 0), (0, LANE - ML_GATES)))
    mqk, mv, og, mif, cq, gates = _misc_proj(
        x1, gn, col(3, 4), col(5), col(6), w_if, b_if, col(8), row(p["cx_q_gain"]), col(9),
        row(p["mix_gate_b"]))

    o_att, lse_att = [], []
    for g, (_, dil) in enumerate(DIL_GROUPS):
        o, lse = _dil_attn(qkv[g], qkv[3 + g], qkv[6 + g], dil=dil)
        o_att.append(o)
        lse_att.append(lse)

    gates_row = jnp.swapaxes(mif.reshape(batch, seq, LANE)[:, :, :ML_GATES], 1, 2)
    y_ml = _mlstm(mqk, mv, og, mif, gates_row, p["ml_conv_w"].astype(F32), row(p["ml_conv_b"]),
                  row(p["ml_out_gain"]), batch=batch, seq=seq)

    ck, cv = _mem_kv(mem, row(p["norm_mem"]), bf(p["w_mem_kv"]), row(p["cx_k_gain"]))

    lanes = jnp.arange(ATT_OUT) // ATT_HEAD_DIM * LSE_LANES_PER_HEAD
    expand = (jnp.arange(LANE)[:, None] == lanes[None, :]).astype(BF16)
    x2 = _merge(x1, o_att, lse_att, y_ml, cq, gates, ck, cv, bf(p["w_br_att"]), bf(p["w_br_ml"]),
                bf(p["w_br_cx"]), bf(p["w_out"]), expand, seq=seq)

    x3 = _ffn(x2, row(p["norm_ffn2"]), bf(p["w_ffn2_in"]), bf(p["w_ffn2_out"]), row(p["norm_final"]),
              final_norm=True)
    return x3.reshape(batch, seq, dm)


def kernel(x, mem, norm_ffn1, w_ffn1_in, w_ffn1_out, norm_mix, norm_mem, w_in, att_q_gain, att_k_gain,
           ml_conv_w, ml_conv_b, ml_gate_b, ml_out_gain, cx_q_gain, cx_k_gain, w_mem_kv, mix_gate_b,
           w_br_att, w_br_ml, w_br_cx, w_out, norm_ffn2, w_ffn2_in, w_ffn2_out, norm_final):
    params = dict(norm_ffn1=norm_ffn1, w_ffn1_in=w_ffn1_in, w_ffn1_out=w_ffn1_out, norm_mix=norm_mix,
                  norm_mem=norm_mem, w_in=w_in, att_q_gain=att_q_gain, att_k_gain=att_k_gain,
                  ml_conv_w=ml_conv_w, ml_conv_b=ml_conv_b, ml_gate_b=ml_gate_b, ml_out_gain=ml_out_gain,
                  cx_q_gain=cx_q_gain, cx_k_gain=cx_k_gain, w_mem_kv=w_mem_kv, mix_gate_b=mix_gate_b,
                  w_br_att=w_br_att, w_br_ml=w_br_ml, w_br_cx=w_br_cx, w_out=w_out, norm_ffn2=norm_ffn2,
                  w_ffn2_in=w_ffn2_in, w_ffn2_out=w_ffn2_out, norm_final=norm_final)
    depth = norm_ffn1.shape[0]
    for layer in range(depth):
        x = _layer(x, mem, {k: v[layer] for k, v in params.items()})
    return x
```

```python
import functools

import jax
import jax.numpy as jnp
from jax import lax
from jax.experimental import pallas as pl
from jax.experimental.pallas import tpu as pltpu

F32 = jnp.float32
BF16 = jnp.bfloat16

EPS = 1e-6
ROPE_THETA = 10000.0
NEG = -1e30

ATT_HEAD_DIM = 64
ATT_SLOTS = 8
DIL_GROUPS = ((128, 1), (512, 4), (2048, 16))
ATT_OUT = ATT_SLOTS * ATT_HEAD_DIM
ATT_QKV = len(DIL_GROUPS) * ATT_OUT
ML_HEADS = 4
ML_HEAD_DIM = 128
ML_WIDTH = ML_HEADS * ML_HEAD_DIM
ML_GATES = 4 * ML_HEADS
ML_CHUNK = 128
ML_CONV = 5
CX_HEADS = 4
CX_HEAD_DIM = 128
CX_WIDTH = CX_HEADS * CX_HEAD_DIM
N_BRANCH = 3

LANE = 128
MXU_DIM = 256
VMEM_LIMIT = 56 * 1024 * 1024

LSE_LANES_PER_HEAD = LANE // ATT_SLOTS
QKV_TILE = 256
ATTN_KEYS_PER_ITER = 512
BAND = 64
assert all((w // 2) // r == BAND for w, r in DIL_GROUPS)

NT_DIMS = (((1,), (1,)), ((), ()))
TN_DIMS = (((0,), (0,)), ((), ()))


def _dot(a, b):
    return jnp.dot(a, b, preferred_element_type=F32)


def _dot_nt(a, b):
    return lax.dot_general(a, b, NT_DIMS, preferred_element_type=F32)


def _dot_exact(a, b):
    return jnp.dot(a, b, preferred_element_type=F32, precision=lax.Precision.HIGHEST)


def _rms(x):
    return x * lax.rsqrt(jnp.mean(x * x, axis=-1, keepdims=True) + EPS)


def _sigmoid(x):
    return 1.0 / (1.0 + jnp.exp(-x))


def _log_sigmoid(x):
    return jnp.minimum(x, 0.0) - jnp.log(1.0 + jnp.exp(-jnp.abs(x)))


def _params(*sem):
    return pltpu.CompilerParams(dimension_semantics=sem, vmem_limit_bytes=VMEM_LIMIT)


def _resident(shape):
    return pl.BlockSpec(shape, lambda *_: (0,) * len(shape), pipeline_mode=pl.Buffered(1))


def _ffn_body(x_ref, g_ref, wg_ref, wu_ref, wo_ref, gf_ref, o_ref, h_ref, a_ref, *, final_norm):
    h_ref[...] = (_rms(x_ref[...]) * g_ref[...]).astype(BF16)
    nj, _, tf = wg_ref.shape
    for j in range(nj):
        h = h_ref[...]
        gate = _dot(h, wg_ref[j])
        up = _dot(h, wu_ref[j])
        a_ref[:, j * tf:(j + 1) * tf] = (gate * _sigmoid(gate) * up).astype(BF16)
    y = x_ref[...] + 0.5 * _dot(a_ref[...], wo_ref[...])
    if final_norm:
        y = _rms(y) * gf_ref[...]
    o_ref[...] = y


def _ffn(x, gain, w_gate, w_up, w_out, final_gain, *, final_norm, tm=512):
    t, dm = x.shape
    d_ff = w_out.shape[0]
    return pl.pallas_call(
        functools.partial(_ffn_body, final_norm=final_norm),
        grid=(t // tm,),
        in_specs=[pl.BlockSpec((tm, dm), lambda i: (i, 0)), _resident((1, dm)), _resident(w_gate.shape),
                  _resident(w_up.shape), _resident(w_out.shape), _resident((1, dm))],
        out_specs=pl.BlockSpec((tm, dm), lambda i: (i, 0)),
        out_shape=jax.ShapeDtypeStruct((t, dm), F32),
        scratch_shapes=[pltpu.VMEM((tm, dm), BF16), pltpu.VMEM((tm, d_ff), BF16)],
        compiler_params=_params("parallel"),
        name="ffn_final" if final_norm else "ffn",
    )(x, gain, w_gate, w_up, w_out, final_gain)


def _ffn_weights(w_in, w_out, tf=MXU_DIM):
    dm, two_ff = w_in.shape
    chunks = w_in.astype(BF16).reshape(dm, 2, two_ff // (2 * tf), tf).transpose(1, 2, 0, 3)
    return chunks[0], chunks[1], w_out.astype(BF16)


def _qkv_body(x_ref, gn_ref, wq_ref, wk_ref, wv_ref, gq_ref, gk_ref, cos0_ref, sin0_ref, cos1_ref, sin1_ref,
              cos2_ref, sin2_ref, e_ref, *rest):
    outs, (hs_ref, hp_ref) = rest[:9], rest[9:]
    q_outs, k_outs, v_outs = outs[0:3], outs[3:6], outs[6:9]
    tables = ((cos0_ref, sin0_ref), (cos1_ref, sin1_ref), (cos2_ref, sin2_ref))
    tm, dm = x_ref.shape
    hn = _rms(x_ref[...]) * gn_ref[...]
    for sl in range(dm // LANE):
        hs_ref[sl] = hn[:, sl * LANE:(sl + 1) * LANE]
    lane = lax.broadcasted_iota(jnp.int32, (tm, ATT_OUT), 1)
    first_half = (lane & (ATT_HEAD_DIM - 1)) < ATT_HEAD_DIM // 2
    e = e_ref[...]
    for g, (_, dil) in enumerate(DIL_GROUPS):
        m = tm // dil
        if dil == 1:
            h = hn.astype(BF16)
        else:
            for c in range(dil):
                for sl in range(dm // LANE):
                    hp_ref[c * m:(c + 1) * m, sl * LANE:(sl + 1) * LANE] = (
                        hs_ref[sl, pl.ds(c, m, stride=dil), :].astype(BF16))
            h = hp_ref[...]
        cos = tables[g][0][...]
        sin = tables[g][1][...]
        cs = slice(g * ATT_OUT, (g + 1) * ATT_OUT)

        def emit(out, val):
            for c in range(dil):
                out[0, c] = val[c * m:(c + 1) * m, :].astype(BF16)

        for w_ref, gain_ref, scale, out in ((wq_ref, gq_ref, ATT_HEAD_DIM ** -0.5, q_outs[g]),
                                            (wk_ref, gk_ref, 1.0, k_outs[g])):
            t = _dot(h, w_ref[:, cs])
            sq = (t * t).astype(BF16)
            ms = jnp.concatenate([_dot(sq[:, c:c + MXU_DIM], e) for c in range(0, ATT_OUT, MXU_DIM)], axis=1)
            tn = t * lax.rsqrt(ms + EPS) * gain_ref[...]
            partner = jnp.where(first_half,
                                pltpu.roll(tn, ATT_OUT - ATT_HEAD_DIM // 2, 1),
                                pltpu.roll(tn, ATT_HEAD_DIM // 2, 1))
            emit(out, (tn * cos + partner * sin) * scale)
        emit(v_outs[g], _dot(h, wv_ref[:, cs]))


def _qkv_proj(x, gn, wq, wk, wv, gq, gk, tables, e64, *, batch, seq, tm=256):
    t, dm = x.shape
    nseq = seq // tm
    pos = pl.BlockSpec((tm, ATT_OUT), lambda i: (i % nseq, 0))
    out_specs, out_shape = [], []
    for _ in range(3):
        for _, dil in DIL_GROUPS:
            out_specs.append(pl.BlockSpec((1, dil, tm // dil, ATT_OUT), lambda i: (i // nseq, 0, i % nseq, 0)))
            out_shape.append(jax.ShapeDtypeStruct((batch, dil, seq // dil, ATT_OUT), BF16))
    return pl.pallas_call(
        _qkv_body,
        grid=(t // tm,),
        in_specs=[pl.BlockSpec((tm, dm), lambda i: (i, 0)), _resident((1, dm)), _resident((dm, ATT_QKV)),
                  _resident((dm, ATT_QKV)), _resident((dm, ATT_QKV)), _resident((1, ATT_OUT)),
                  _resident((1, ATT_OUT))] + [pos] * 6 + [_resident((MXU_DIM, MXU_DIM))],
        out_specs=out_specs,
        out_shape=out_shape,
        scratch_shapes=[pltpu.VMEM((dm // LANE, tm, LANE), F32),
                        pltpu.VMEM((tm, dm), BF16)],
        compiler_params=_params("parallel"),
        name="qkv_proj",
    )(x, gn, wq, wk, wv, gq, gk, *tables, e64)


def _misc_body(x_ref, gn_ref, wqk_ref, wv_ref, wo_ref, wif_ref, bif_ref, wcq_ref, gcq_ref, wg_ref, bg_ref,
               mqk_ref, mv_ref, og_ref, mif_ref, cq_ref, g_ref):
    h = (_rms(x_ref[...]) * gn_ref[...]).astype(BF16)
    mqk_ref[...] = _dot(h, wqk_ref[...]).astype(BF16)
    mv_ref[...] = _dot(h, wv_ref[...]).astype(BF16)
    og_ref[...] = _sigmoid(_dot(h, wo_ref[...])).astype(BF16)
    mif_ref[...] = _dot(h, wif_ref[...]) + bif_ref[...]
    t = _dot(h, wcq_ref[...])
    for hd in range(CX_HEADS):
        cs = slice(hd * CX_HEAD_DIM, (hd + 1) * CX_HEAD_DIM)
        cq_ref[:, cs] = (_rms(t[:, cs]) * gcq_ref[...] * CX_HEAD_DIM ** -0.5).astype(BF16)
    dm = x_ref.shape[1]
    for br in range(N_BRANCH):
        cs = slice(br * dm, (br + 1) * dm)
        g_ref[:, cs] = _sigmoid(_dot(h, wg_ref[:, cs]) + bg_ref[:, cs]).astype(BF16)


def _misc_proj(x, gn, wqk, wv, wo, wif, bif, wcq, gcq, wg, bg, *, tm=256):
    t, dm = x.shape
    tok = lambda w: pl.BlockSpec((tm, w), lambda i: (i, 0))
    widths = (2 * ML_WIDTH, ML_WIDTH, ML_WIDTH, LANE, CX_WIDTH, N_BRANCH * dm)
    dtypes = (BF16, BF16, BF16, F32, BF16, BF16)
    return pl.pallas_call(
        _misc_body,
        grid=(t // tm,),
        in_specs=[tok(dm), _resident((1, dm)), _resident(wqk.shape), _resident(wv.shape), _resident(wo.shape),
                  _resident(wif.shape), _resident(bif.shape), _resident(wcq.shape), _resident(gcq.shape),
                  _resident(wg.shape), _resident(bg.shape)],
        out_specs=[tok(w) for w in widths],
        out_shape=[jax.ShapeDtypeStruct((t, w), d) for w, d in zip(widths, dtypes)],
        compiler_params=_params("parallel"),
        name="misc_proj",
    )(x, gn, wqk, wv, wo, wif, bif, wcq, gcq, wg, bg)


def _attn_body(q_ref, k_ref, v_ref, o_ref, lse_ref, *, dil, n, bq, win, unroll):
    lane = lax.broadcasted_iota(jnp.int32, (bq, LANE), 1)
    low_head = lane < ATT_HEAD_DIM
    lse_group = lane // LSE_LANES_PER_HEAD
    qi = lax.broadcasted_iota(jnp.int32, (bq, win), 0)
    ki = lax.broadcasted_iota(jnp.int32, (bq, win), 1)

    def where_block(i, u):
        blk = i * unroll + u
        c = blk // nblk
        q0 = pl.multiple_of((blk % nblk) * bq, bq)
        ks = pl.multiple_of(jnp.clip(q0 - BAND, 0, n - win), BAND)
        return c, q0, ks

    def scores(c, q0, ks):
        valid = jnp.abs((q0 + qi) - (ks + ki)) <= BAND
        out = []
        for pair in range(ATT_SLOTS // 2):
            cs = slice(pair * LANE, (pair + 1) * LANE)
            qp = q_ref[0, c, pl.ds(q0, bq), cs]
            kp = k_ref[0, c, pl.ds(ks, win), cs]
            for half in range(2):
                qm = jnp.where(low_head if half == 0 else ~low_head, qp, jnp.zeros_like(qp))
                out.append(jnp.where(valid, _dot_nt(qm, kp), NEG))
        return out

    def finish(c, q0, ks, s_heads):
        lse_row = jnp.zeros((bq, LANE), F32)
        for pair in range(ATT_SLOTS // 2):
            cs = slice(pair * LANE, (pair + 1) * LANE)
            vp = v_ref[0, c, pl.ds(ks, win), cs]
            halves = []
            for half in range(2):
                s = s_heads[2 * pair + half]
                m = jnp.max(s, axis=-1, keepdims=True)
                p = jnp.exp(s - m)
                l = jnp.sum(p, axis=-1, keepdims=True)
                halves.append(_dot(p.astype(BF16), vp) / l)
                lse_row = jnp.where(lse_group == 2 * pair + half, m + jnp.log(l), lse_row)
            o_ref[0, c, pl.ds(q0, bq), cs] = jnp.where(low_head, halves[0], halves[1]).astype(BF16)
        lse_ref[0, c, pl.ds(q0, bq), :] = lse_row

    nblk = n // bq

    def step(i, carry):
        at = [where_block(i, u) for u in range(unroll)]
        s_heads = scores(*at[0])
        for u in range(unroll):
            s_next = scores(*at[u + 1]) if u + 1 < unroll else None
            finish(*at[u], s_heads)
            s_heads = s_next
        return carry
    lax.fori_loop(0, dil * nblk // unroll, step, 0)


def _dil_attn(q, k, v, *, dil, bq=128):
    batch, _, n, _ = q.shape
    win = min(n, bq + 2 * BAND)
    unroll = ATTN_KEYS_PER_ITER // win
    blk = lambda w: pl.BlockSpec((1, dil, n, w), lambda b: (b, 0, 0, 0))
    return pl.pallas_call(
        functools.partial(_attn_body, dil=dil, n=n, bq=bq, win=win, unroll=unroll),
        grid=(batch,),
        in_specs=[blk(ATT_OUT)] * 3,
        out_specs=[blk(ATT_OUT), blk(LANE)],
        out_shape=[jax.ShapeDtypeStruct((batch, dil, n, ATT_OUT), BF16),
                   jax.ShapeDtypeStruct((batch, dil, n, LANE), F32)],
        compiler_params=_params("parallel"),
        name=f"dil_attn_r{dil}",
    )(q, k, v)


def _mlstm_body(qk_ref, v_ref, og_ref, gc_ref, gr_ref, cw_ref, cb_ref, gain_ref, o_ref,
                pad_ref, qs_ref, ks_ref, hf_ref, hb_ref, c_ref, n_ref, m_ref, *, seq, chunk, row_block=256):
    halo = 8
    zeros = jnp.zeros((halo, LANE), F32)
    pad_ref[0:halo, :] = zeros
    pad_ref[seq + halo:seq + 2 * halo, :] = zeros
    nrb = seq // row_block
    for c in range(2 * ML_HEADS):
        cs = slice(c * LANE, (c + 1) * LANE)

        def fill(i, carry):
            r0 = pl.multiple_of(i * row_block, row_block)
            pad_ref[pl.ds(r0 + halo, row_block), :] = qk_ref[0, pl.ds(r0, row_block), cs].astype(F32)
            return carry
        lax.fori_loop(0, nrb, fill, 0)

        def conv(i, carry):
            r0 = pl.multiple_of(i * row_block, row_block)
            acc = jnp.zeros((row_block, LANE), F32) + cb_ref[:, cs]
            for j in range(ML_CONV):
                acc = acc + cw_ref[j:j + 1, cs] * pad_ref[pl.ds(r0 + (halo - ML_CONV // 2 + j), row_block), :]
            y = acc * _sigmoid(acc)
            if c < ML_HEADS:
                qs_ref[pl.ds(r0, row_block), cs] = (y * ML_HEAD_DIM ** -0.5).astype(BF16)
            else:
                ks_ref[pl.ds(r0, row_block), (c - ML_HEADS) * LANE:(c - ML_HEADS + 1) * LANE] = y.astype(BF16)
            return carry
        lax.fori_loop(0, nrb, conv, 0)

    c_ref[...] = jnp.zeros_like(c_ref)
    n_ref[...] = jnp.zeros_like(n_ref)
    m_ref[...] = jnp.zeros_like(m_ref)

    nc = seq // chunk
    ri = lax.broadcasted_iota(jnp.int32, (chunk, chunk), 0)
    ci = lax.broadcasted_iota(jnp.int32, (chunk, chunk), 1)
    lower = ri >= ci
    upper = ri <= ci
    lower_f = lower.astype(F32)
    upper_f = upper.astype(F32)

    def step(it, carry):
        for d in range(2):
            cidx = it if d == 0 else nc - 1 - it
            t0 = pl.multiple_of(cidx * chunk, chunk)
            rows = pl.ds(t0, chunk)
            gcol = gc_ref[0, rows, :]
            grow = gr_ref[0, :, rows]
            if d == 0:
                cum_col = _dot_exact(lower_f, _log_sigmoid(gcol))
                cum_row = _dot_exact(_log_sigmoid(grow), upper_f)
                mask = lower
            else:
                cum_col = _dot_exact(upper_f, _log_sigmoid(gcol))
                cum_row = _dot_exact(_log_sigmoid(grow), lower_f)
                mask = upper
            for hd in range(ML_HEADS):
                st = d * ML_HEADS + hd
                i_gate = (2 * d) * ML_HEADS + hd
                f_gate = (2 * d + 1) * ML_HEADS + hd
                hs = slice(hd * LANE, (hd + 1) * LANE)
                li_col = gcol[:, i_gate:i_gate + 1]
                li_row = grow[i_gate:i_gate + 1, :]
                b_col = cum_col[:, f_gate:f_gate + 1]
                b_row = cum_row[f_gate:f_gate + 1, :]
                gtot = b_col[chunk - 1:chunk, :] if d == 0 else b_col[0:1, :]
                m_prev = m_ref[st][0:1, 0:1]
                n_prev = n_ref[st]
                c_prev = c_ref[st]
                q = qs_ref[rows, hs]
                k = ks_ref[rows, hs]
                v = v_ref[0, rows, hs]

                dmat = jnp.where(mask, b_col - b_row + li_row, NEG)
                inter = b_col + m_prev
                mt = jnp.maximum(jnp.max(dmat, axis=-1, keepdims=True), inter)
                wts = jnp.exp(dmat - mt)
                w_inter = jnp.exp(inter - mt)
                sc = _dot_nt(q, k) * wts
                num = _dot(sc.astype(BF16), v) + w_inter * _dot_nt(q, c_prev.astype(BF16))
                den = (jnp.sum(sc, axis=-1, keepdims=True)
                       + w_inter * jnp.sum(q.astype(F32) * n_prev, axis=-1, keepdims=True))
                hout = num / jnp.maximum(jnp.abs(den), jnp.exp(-mt))
                if d == 0:
                    hf_ref[rows, hs] = hout
                else:
                    hb_ref[rows, hs] = hout

                a_col = gtot - b_col + li_col
                a_row = gtot - b_row + li_row
                m_new = jnp.maximum(gtot + m_prev, jnp.max(a_row, axis=-1, keepdims=True))
                decay = jnp.exp(gtot + m_prev - m_new)
                wk = jnp.exp(a_col - m_new)
                vw = (v.astype(F32) * wk).astype(BF16)
                c_ref[st] = decay * c_prev + lax.dot_general(vw, k, TN_DIMS, preferred_element_type=F32)
                n_ref[st] = decay * n_prev + jnp.sum(k.astype(F32) * wk, axis=0, keepdims=True)
                m_ref[st] = jnp.broadcast_to(m_new, m_ref.shape[1:])
        return carry

    lax.fori_loop(0, nc, step, 0)

    def finish(i, carry):
        rows = pl.ds(pl.multiple_of(i * row_block, row_block), row_block)
        for hd in range(ML_HEADS):
            hs = slice(hd * LANE, (hd + 1) * LANE)
            hm = _rms(hf_ref[rows, hs] + hb_ref[rows, hs]) * gain_ref[:, hs]
            o_ref[0, rows, hs] = (hm * og_ref[0, rows, hs].astype(F32)).astype(BF16)
        return carry
    lax.fori_loop(0, nrb, finish, 0)


def _mlstm(mqk, mv, og, gates_col, gates_row, conv_w, conv_b, out_gain, *, batch, seq, chunk=ML_CHUNK):
    halo = 8
    b3 = lambda a: a.reshape(batch, seq, a.shape[-1])
    per_b = lambda w: pl.BlockSpec((1, seq, w), lambda b: (b, 0, 0))
    y = pl.pallas_call(
        functools.partial(_mlstm_body, seq=seq, chunk=chunk),
        grid=(batch,),
        in_specs=[per_b(2 * ML_WIDTH), per_b(ML_WIDTH), per_b(ML_WIDTH), per_b(LANE),
                  pl.BlockSpec((1, ML_GATES, seq), lambda b: (b, 0, 0)),
                  _resident(conv_w.shape), _resident(conv_b.shape), _resident(out_gain.shape)],
        out_specs=per_b(ML_WIDTH),
        out_shape=jax.ShapeDtypeStruct((batch, seq, ML_WIDTH), BF16),
        scratch_shapes=[
            pltpu.VMEM((seq + 2 * halo, LANE), F32),
            pltpu.VMEM((seq, ML_WIDTH), BF16),
            pltpu.VMEM((seq, ML_WIDTH), BF16),
            pltpu.VMEM((seq, ML_WIDTH), F32),
            pltpu.VMEM((seq, ML_WIDTH), F32),
            pltpu.VMEM((2 * ML_HEADS, ML_HEAD_DIM, ML_HEAD_DIM), F32),
            pltpu.VMEM((2 * ML_HEADS, 1, ML_HEAD_DIM), F32),
            pltpu.VMEM((2 * ML_HEADS, 8, LANE), F32),
        ],
        compiler_params=_params("parallel"),
        name="mlstm",
    )(b3(mqk), b3(mv), b3(og), b3(gates_col), gates_row, conv_w, conv_b, out_gain)
    return y.reshape(batch * seq, ML_WIDTH)


def _mem_kv_body(mem_ref, gn_ref, w_ref, gk_ref, ck_ref, cv_ref):
    h = (_rms(mem_ref[0]) * gn_ref[...]).astype(BF16)
    kv = _dot(h, w_ref[...])
    for hd in range(CX_HEADS):
        cs = slice(hd * CX_HEAD_DIM, (hd + 1) * CX_HEAD_DIM)
        ck_ref[0, :, cs] = (_rms(kv[:, cs]) * gk_ref[...]).astype(BF16)
    cv_ref[0] = kv[:, CX_WIDTH:].astype(BF16)


def _mem_kv(mem, gn, w, gk):
    b, mlen, dm = mem.shape
    blk = lambda w_: pl.BlockSpec((1, mlen, w_), lambda i: (i, 0, 0))
    return pl.pallas_call(
        _mem_kv_body,
        grid=(b,),
        in_specs=[blk(dm), _resident((1, dm)), _resident(w.shape), _resident(gk.shape)],
        out_specs=[blk(CX_WIDTH)] * 2,
        out_shape=[jax.ShapeDtypeStruct((b, mlen, CX_WIDTH), BF16)] * 2,
        compiler_params=_params("parallel"),
        name="mem_kv",
    )(mem, gn, w, gk)


def _merge_body(x_ref, o0_ref, o1_ref, o2_ref, l0_ref, l1_ref, l2_ref, yml_ref, cq_ref, g_ref, ck_ref, cv_ref,
                wa_ref, wm_ref, wc_ref, wout_ref, ex_ref, out_ref, nat_ref):
    tm, dm = x_ref.shape
    nslab = ATT_OUT // LANE

    def natural(ref, dil, width):
        if dil == 1:
            return ref[0, 0].astype(F32)
        m = tm // dil
        for c in range(dil):
            for sl in range(width // LANE):
                nat_ref[sl, pl.ds(c, m, stride=dil), :] = ref[0, c, :, sl * LANE:(sl + 1) * LANE].astype(F32)
        return jnp.concatenate([nat_ref[sl] for sl in range(width // LANE)], axis=1)

    dils = [dil for _, dil in DIL_GROUPS]
    lses = [natural(ref, dil, LANE) for ref, dil in zip((l0_ref, l1_ref, l2_ref), dils)]
    top = jnp.maximum(jnp.maximum(lses[0], lses[1]), lses[2])
    es = [jnp.exp(l - top) for l in lses]
    inv = 1.0 / (es[0] + es[1] + es[2])
    ex = ex_ref[...]
    y_att = jnp.zeros((tm, ATT_OUT), F32)
    for e, o_ref, dil in zip(es, (o0_ref, o1_ref, o2_ref), dils):
        alpha = e * inv
        hi = alpha.astype(BF16)
        lo = (alpha - hi.astype(F32)).astype(BF16)
        y_att = y_att + (_dot(hi, ex) + _dot(lo, ex)) * natural(o_ref, dil, nslab * LANE)

    y_cx = []
    for hd in range(CX_HEADS):
        cs = slice(hd * CX_HEAD_DIM, (hd + 1) * CX_HEAD_DIM)
        s = _dot_nt(cq_ref[:, cs], ck_ref[0, :, cs])
        p = jnp.exp(s - jnp.max(s, axis=-1, keepdims=True))
        l = jnp.sum(p, axis=-1, keepdims=True)
        y_cx.append(_dot(p.astype(BF16), cv_ref[0, :, cs]) / l)
    y_cx = jnp.concatenate(y_cx, axis=1)

    merged = (g_ref[:, 0:dm].astype(F32) * _dot(y_att.astype(BF16), wa_ref[...])
              + g_ref[:, dm:2 * dm].astype(F32) * _dot(yml_ref[...], wm_ref[...])
              + g_ref[:, 2 * dm:3 * dm].astype(F32) * _dot(y_cx.astype(BF16), wc_ref[...]))
    out_ref[...] = x_ref[...] + _dot(merged.astype(BF16), wout_ref[...])


def _merge(x, o_att, lse_att, y_ml, cq, gates, ck, cv, wa, wm, wc, wout, expand, *, seq, tm=256):
    t, dm = x.shape
    mlen = ck.shape[1]
    nseq = seq // tm
    tok = lambda w: pl.BlockSpec((tm, w), lambda i: (i, 0))
    mem = pl.BlockSpec((1, mlen, CX_WIDTH), lambda i: (i // nseq, 0, 0))
    res = lambda dil, w: pl.BlockSpec((1, dil, tm // dil, w), lambda i: (i // nseq, 0, i % nseq, 0))
    dils = [dil for _, dil in DIL_GROUPS]
    return pl.pallas_call(
        _merge_body,
        grid=(t // tm,),
        in_specs=[tok(dm)] + [res(dil, ATT_OUT) for dil in dils] + [res(dil, LANE) for dil in dils]
                 + [tok(ML_WIDTH), tok(CX_WIDTH), tok(N_BRANCH * dm), mem, mem,
                    _resident(wa.shape), _resident(wm.shape), _resident(wc.shape), _resident(wout.shape),
                    _resident(expand.shape)],
        out_specs=tok(dm),
        out_shape=jax.ShapeDtypeStruct((t, dm), F32),
        scratch_shapes=[pltpu.VMEM((ATT_OUT // LANE, tm, LANE), F32)],
        compiler_params=_params("parallel"),
        name="merge",
    )(x, *o_att, *lse_att, y_ml, cq, gates, ck, cv, wa, wm, wc, wout, expand)


def _rope_tables(seq, tm):
    d = ATT_HEAD_DIM
    inv = ROPE_THETA ** (-jnp.arange(0, d, 2, dtype=F32) / d)
    ang = jnp.arange(seq, dtype=F32)[:, None] * inv[None, :]
    cos, sin = jnp.cos(ang), jnp.sin(ang)
    cos_t = jnp.tile(jnp.concatenate([cos, cos], axis=-1), (1, ATT_SLOTS))
    sin_t = jnp.tile(jnp.concatenate([-sin, sin], axis=-1), (1, ATT_SLOTS))
    tables = []
    for _, dil in DIL_GROUPS:
        perm = lambda a: a.reshape(seq // tm, tm // dil, dil, -1).swapaxes(1, 2).reshape(seq, -1)
        tables += [perm(cos_t), perm(sin_t)]
    return tables


def _layer(x, mem, p):
    batch, seq, dm = x.shape
    xt = x.reshape(batch * seq, dm)
    row = lambda a: a.reshape(1, -1).astype(F32)
    bf = lambda a: a.astype(BF16)

    x1 = _ffn(xt, row(p["norm_ffn1"]), *_ffn_weights(p["w_ffn1_in"], p["w_ffn1_out"]), row(p["norm_final"]),
              final_norm=False)

    w_in = bf(p["w_in"])
    sizes = (ATT_QKV, ATT_QKV, ATT_QKV, ML_WIDTH, ML_WIDTH, ML_WIDTH, ML_WIDTH, ML_GATES, CX_WIDTH, N_BRANCH * dm)
    offs = [0]
    for s in sizes:
        offs.append(offs[-1] + s)
    col = lambda i, j=None: w_in[:, offs[i]:offs[(i if j is None else j) + 1]]
    gn = row(p["norm_mix"])

    idx = jnp.arange(MXU_DIM) // ATT_HEAD_DIM
    e64 = jnp.where(idx[:, None] == idx[None, :], 1.0 / ATT_HEAD_DIM, 0.0).astype(BF16)
    tile8 = lambda a: jnp.tile(row(a), (1, ATT_SLOTS))
    qkv = _qkv_proj(x1, gn, col(0), col(1), col(2), tile8(p["att_q_gain"]), tile8(p["att_k_gain"]),
                    _rope_tables(seq, QKV_TILE), e64, batch=batch, seq=seq, tm=QKV_TILE)

    w_if = jnp.pad(col(7), ((0, 0), (0, LANE - ML_GATES)))
    b_if = jnp.pad(row(p["ml_gate_b"]), ((0, 0), (0, LANE - ML_GATES)))
    mqk, mv, og, mif, cq, gates = _misc_proj(
        x1, gn, col(3, 4), col(5), col(6), w_if, b_if, col(8), row(p["cx_q_gain"]), col(9),
        row(p["mix_gate_b"]))

    o_att, lse_att = [], []
    for g, (_, dil) in enumerate(DIL_GROUPS):
        o, lse = _dil_attn(qkv[g], qkv[3 + g], qkv[6 + g], dil=dil)
        o_att.append(o)
        lse_att.append(lse)

    gates_row = jnp.swapaxes(mif.reshape(batch, seq, LANE)[:, :, :ML_GATES], 1, 2)
    y_ml = _mlstm(mqk, mv, og, mif, gates_row, p["ml_conv_w"].astype(F32), row(p["ml_conv_b"]),
                  row(p["ml_out_gain"]), batch=batch, seq=seq)

    ck, cv = _mem_kv(mem, row(p["norm_mem"]), bf(p["w_mem_kv"]), row(p["cx_k_gain"]))

    lanes = jnp.arange(ATT_OUT) // ATT_HEAD_DIM * LSE_LANES_PER_HEAD
    expand = (jnp.arange(LANE)[:, None] == lanes[None, :]).astype(BF16)
    x2 = _merge(x1, o_att, lse_att, y_ml, cq, gates, ck, cv, bf(p["w_br_att"]), bf(p["w_br_ml"]),
                bf(p["w_br_cx"]), bf(p["w_out"]), expand, seq=seq)

    x3 = _ffn(x2, row(p["norm_ffn2"]), *_ffn_weights(p["w_ffn2_in"], p["w_ffn2_out"]), row(p["norm_final"]),
              final_norm=True)
    return x3.reshape(batch, seq, dm)


def kernel(x, mem, norm_ffn1, w_ffn1_in, w_ffn1_out, norm_mix, norm_mem, w_in, att_q_gain, att_k_gain,
           ml_conv_w, ml_conv_b, ml_gate_b, ml_out_gain, cx_q_gain, cx_k_gain, w_mem_kv, mix_gate_b,
           w_br_att, w_br_ml, w_br_cx, w_out, norm_ffn2, w_ffn2_in, w_ffn2_out, norm_final):
    params = dict(norm_ffn1=norm_ffn1, w_ffn1_in=w_ffn1_in, w_ffn1_out=w_ffn1_out, norm_mix=norm_mix,
                  norm_mem=norm_mem, w_in=w_in, att_q_gain=att_q_gain, att_k_gain=att_k_gain,
                  ml_conv_w=ml_conv_w, ml_conv_b=ml_conv_b, ml_gate_b=ml_gate_b, ml_out_gain=ml_out_gain,
                  cx_q_gain=cx_q_gain, cx_k_gain=cx_k_gain, w_mem_kv=w_mem_kv, mix_gate_b=mix_gate_b,
                  w_br_att=w_br_att, w_br_ml=w_br_ml, w_br_cx=w_br_cx, w_out=w_out, norm_ffn2=norm_ffn2,
                  w_ffn2_in=w_ffn2_in, w_ffn2_out=w_ffn2_out, norm_final=norm_final)
    depth = norm_ffn1.shape[0]
    for layer in range(depth):
        x = _layer(x, mem, {k: v[layer] for k, v in params.items()})
    return x
```

```python
import functools

import jax
import jax.numpy as jnp
from jax import lax
from jax.experimental import pallas as pl
from jax.experimental.pallas import tpu as pltpu

F32 = jnp.float32
BF16 = jnp.bfloat16

EPS = 1e-6
ROPE_THETA = 10000.0
NEG = -1e30

ATT_HEAD_DIM = 64
ATT_SLOTS = 8
DIL_GROUPS = ((128, 1), (512, 4), (2048, 16))
ATT_OUT = ATT_SLOTS * ATT_HEAD_DIM
ATT_QKV = len(DIL_GROUPS) * ATT_OUT
ML_HEADS = 4
ML_HEAD_DIM = 128
ML_WIDTH = ML_HEADS * ML_HEAD_DIM
ML_GATES = 4 * ML_HEADS
ML_CHUNK = 128
ML_CONV = 5
CX_HEADS = 4
CX_HEAD_DIM = 128
CX_WIDTH = CX_HEADS * CX_HEAD_DIM
N_BRANCH = 3

LANE = 128
MXU_DIM = 256
VMEM_LIMIT = 56 * 1024 * 1024

LSE_LANES_PER_HEAD = LANE // ATT_SLOTS
QKV_TILE = 256
ATTN_KEYS_PER_ITER = 512
BAND = 64
assert all((w // 2) // r == BAND for w, r in DIL_GROUPS)

NT_DIMS = (((1,), (1,)), ((), ()))
TN_DIMS = (((0,), (0,)), ((), ()))


def _dot(a, b):
    return jnp.dot(a, b, preferred_element_type=F32)


def _dot_nt(a, b):
    return lax.dot_general(a, b, NT_DIMS, preferred_element_type=F32)


def _dot_exact(a, b):
    return jnp.dot(a, b, preferred_element_type=F32, precision=lax.Precision.HIGHEST)


def _rms(x):
    return x * lax.rsqrt(jnp.mean(x * x, axis=-1, keepdims=True) + EPS)


def _sigmoid(x):
    return 1.0 / (1.0 + jnp.exp(-x))


def _log_sigmoid(x):
    return jnp.minimum(x, 0.0) - jnp.log(1.0 + jnp.exp(-jnp.abs(x)))


def _params(*sem):
    return pltpu.CompilerParams(dimension_semantics=sem, vmem_limit_bytes=VMEM_LIMIT)


def _resident(shape):
    return pl.BlockSpec(shape, lambda *_: (0,) * len(shape), pipeline_mode=pl.Buffered(1))


def _ffn_body(x_ref, g_ref, wg_ref, wu_ref, wo_ref, gf_ref, o_ref, h_ref, a_ref, *, final_norm):
    h_ref[...] = (_rms(x_ref[...]) * g_ref[...]).astype(BF16)
    nj, _, tf = wg_ref.shape
    for j in range(nj):
        h = h_ref[...]
        gate = _dot(h, wg_ref[j])
        up = _dot(h, wu_ref[j])
        a_ref[:, j * tf:(j + 1) * tf] = (gate * _sigmoid(gate) * up).astype(BF16)
    y = x_ref[...] + 0.5 * _dot(a_ref[...], wo_ref[...])
    if final_norm:
        y = _rms(y) * gf_ref[...]
    o_ref[...] = y


def _ffn(x, gain, w_gate, w_up, w_out, final_gain, *, final_norm, tm=512):
    t, dm = x.shape
    d_ff = w_out.shape[0]
    return pl.pallas_call(
        functools.partial(_ffn_body, final_norm=final_norm),
        grid=(t // tm,),
        in_specs=[pl.BlockSpec((tm, dm), lambda i: (i, 0)), _resident((1, dm)), _resident(w_gate.shape),
                  _resident(w_up.shape), _resident(w_out.shape), _resident((1, dm))],
        out_specs=pl.BlockSpec((tm, dm), lambda i: (i, 0)),
        out_shape=jax.ShapeDtypeStruct((t, dm), F32),
        scratch_shapes=[pltpu.VMEM((tm, dm), BF16), pltpu.VMEM((tm, d_ff), BF16)],
        compiler_params=_params("parallel"),
        name="ffn_final" if final_norm else "ffn",
    )(x, gain, w_gate, w_up, w_out, final_gain)


def _ffn_weights(w_in, w_out, tf=MXU_DIM):
    dm, two_ff = w_in.shape
    chunks = w_in.astype(BF16).reshape(dm, 2, two_ff // (2 * tf), tf).transpose(1, 2, 0, 3)
    return chunks[0], chunks[1], w_out.astype(BF16)


def _qkv_body(x_ref, gn_ref, wq_ref, wk_ref, wv_ref, gq_ref, gk_ref, cos0_ref, sin0_ref, cos1_ref, sin1_ref,
              cos2_ref, sin2_ref, e_ref, *rest):
    outs, (hs_ref, hp_ref) = rest[:9], rest[9:]
    q_outs, k_outs, v_outs = outs[0:3], outs[3:6], outs[6:9]
    tables = ((cos0_ref, sin0_ref), (cos1_ref, sin1_ref), (cos2_ref, sin2_ref))
    tm, dm = x_ref.shape
    hn = _rms(x_ref[...]) * gn_ref[...]
    for sl in range(dm // LANE):
        hs_ref[sl] = hn[:, sl * LANE:(sl + 1) * LANE]
    lane = lax.broadcasted_iota(jnp.int32, (tm, ATT_OUT), 1)
    first_half = (lane & (ATT_HEAD_DIM - 1)) < ATT_HEAD_DIM // 2
    e = e_ref[...]
    for g, (_, dil) in enumerate(DIL_GROUPS):
        m = tm // dil
        if dil == 1:
            h = hn.astype(BF16)
        else:
            for c in range(dil):
                for sl in range(dm // LANE):
                    hp_ref[c * m:(c + 1) * m, sl * LANE:(sl + 1) * LANE] = (
                        hs_ref[sl, pl.ds(c, m, stride=dil), :].astype(BF16))
            h = hp_ref[...]
        cos = tables[g][0][...]
        sin = tables[g][1][...]
        cs = slice(g * ATT_OUT, (g + 1) * ATT_OUT)

        def emit(out, val):
            for c in range(dil):
                out[0, c] = val[c * m:(c + 1) * m, :].astype(BF16)

        for w_ref, gain_ref, scale, out in ((wq_ref, gq_ref, ATT_HEAD_DIM ** -0.5, q_outs[g]),
                                            (wk_ref, gk_ref, 1.0, k_outs[g])):
            t = _dot(h, w_ref[:, cs])
            sq = (t * t).astype(BF16)
            ms = jnp.concatenate([_dot(sq[:, c:c + MXU_DIM], e) for c in range(0, ATT_OUT, MXU_DIM)], axis=1)
            tn = t * lax.rsqrt(ms + EPS) * gain_ref[...]
            partner = jnp.where(first_half,
                                pltpu.roll(tn, ATT_OUT - ATT_HEAD_DIM // 2, 1),
                                pltpu.roll(tn, ATT_HEAD_DIM // 2, 1))
            emit(out, (tn * cos + partner * sin) * scale)
        emit(v_outs[g], _dot(h, wv_ref[:, cs]))


def _qkv_proj(x, gn, wq, wk, wv, gq, gk, tables, e64, *, batch, seq, tm=256):
    t, dm = x.shape
    nseq = seq // tm
    pos = pl.BlockSpec((tm, ATT_OUT), lambda i: (i % nseq, 0))
    out_specs, out_shape = [], []
    for _ in range(3):
        for _, dil in DIL_GROUPS:
            out_specs.append(pl.BlockSpec((1, dil, tm // dil, ATT_OUT), lambda i: (i // nseq, 0, i % nseq, 0)))
            out_shape.append(jax.ShapeDtypeStruct((batch, dil, seq // dil, ATT_OUT), BF16))
    return pl.pallas_call(
        _qkv_body,
        grid=(t // tm,),
        in_specs=[pl.BlockSpec((tm, dm), lambda i: (i, 0)), _resident((1, dm)), _resident((dm, ATT_QKV)),
                  _resident((dm, ATT_QKV)), _resident((dm, ATT_QKV)), _resident((1, ATT_OUT)),
                  _resident((1, ATT_OUT))] + [pos] * 6 + [_resident((MXU_DIM, MXU_DIM))],
        out_specs=out_specs,
        out_shape=out_shape,
        scratch_shapes=[pltpu.VMEM((dm // LANE, tm, LANE), F32),
                        pltpu.VMEM((tm, dm), BF16)],
        compiler_params=_params("parallel"),
        name="qkv_proj",
    )(x, gn, wq, wk, wv, gq, gk, *tables, e64)


def _misc_body(x_ref, gn_ref, wqk_ref, wv_ref, wo_ref, wif_ref, bif_ref, wcq_ref, gcq_ref, wg_ref, bg_ref,
               mqk_ref, mv_ref, og_ref, mif_ref, cq_ref, g_ref):
    h = (_rms(x_ref[...]) * gn_ref[...]).astype(BF16)
    mqk_ref[...] = _dot(h, wqk_ref[...]).astype(BF16)
    mv_ref[...] = _dot(h, wv_ref[...]).astype(BF16)
    og_ref[...] = _sigmoid(_dot(h, wo_ref[...])).astype(BF16)
    mif_ref[...] = _dot(h, wif_ref[...]) + bif_ref[...]
    t = _dot(h, wcq_ref[...])
    for hd in range(CX_HEADS):
        cs = slice(hd * CX_HEAD_DIM, (hd + 1) * CX_HEAD_DIM)
        cq_ref[:, cs] = (_rms(t[:, cs]) * gcq_ref[...] * CX_HEAD_DIM ** -0.5).astype(BF16)
    dm = x_ref.shape[1]
    for br in range(N_BRANCH):
        cs = slice(br * dm, (br + 1) * dm)
        g_ref[:, cs] = _sigmoid(_dot(h, wg_ref[:, cs]) + bg_ref[:, cs]).astype(BF16)


def _misc_proj(x, gn, wqk, wv, wo, wif, bif, wcq, gcq, wg, bg, *, tm=256):
    t, dm = x.shape
    tok = lambda w: pl.BlockSpec((tm, w), lambda i: (i, 0))
    widths = (2 * ML_WIDTH, ML_WIDTH, ML_WIDTH, LANE, CX_WIDTH, N_BRANCH * dm)
    dtypes = (BF16, BF16, BF16, F32, BF16, BF16)
    return pl.pallas_call(
        _misc_body,
        grid=(t // tm,),
        in_specs=[tok(dm), _resident((1, dm)), _resident(wqk.shape), _resident(wv.shape), _resident(wo.shape),
                  _resident(wif.shape), _resident(bif.shape), _resident(wcq.shape), _resident(gcq.shape),
                  _resident(wg.shape), _resident(bg.shape)],
        out_specs=[tok(w) for w in widths],
        out_shape=[jax.ShapeDtypeStruct((t, w), d) for w, d in zip(widths, dtypes)],
        compiler_params=_params("parallel"),
        name="misc_proj",
    )(x, gn, wqk, wv, wo, wif, bif, wcq, gcq, wg, bg)


def _attn_body(q_ref, k_ref, v_ref, o_ref, lse_ref, *, dil, n, bq, win, unroll):
    lane = lax.broadcasted_iota(jnp.int32, (bq, LANE), 1)
    low_head = lane < ATT_HEAD_DIM
    lse_group = lane // LSE_LANES_PER_HEAD
    qi = lax.broadcasted_iota(jnp.int32, (bq, win), 0)
    ki = lax.broadcasted_iota(jnp.int32, (bq, win), 1)

    def where_block(i, u):
        blk = i * unroll + u
        c = blk // nblk
        q0 = pl.multiple_of((blk % nblk) * bq, bq)
        ks = pl.multiple_of(jnp.clip(q0 - BAND, 0, n - win), BAND)
        return c, q0, ks

    def scores(c, q0, ks):
        valid = jnp.abs((q0 + qi) - (ks + ki)) <= BAND
        out = []
        for pair in range(ATT_SLOTS // 2):
            cs = slice(pair * LANE, (pair + 1) * LANE)
            qp = q_ref[0, c, pl.ds(q0, bq), cs]
            kp = k_ref[0, c, pl.ds(ks, win), cs]
            for half in range(2):
                qm = jnp.where(low_head if half == 0 else ~low_head, qp, jnp.zeros_like(qp))
                out.append(jnp.where(valid, _dot_nt(qm, kp), NEG))
        return out

    def finish(c, q0, ks, s_heads):
        lse_row = jnp.zeros((bq, LANE), F32)
        for pair in range(ATT_SLOTS // 2):
            cs = slice(pair * LANE, (pair + 1) * LANE)
            vp = v_ref[0, c, pl.ds(ks, win), cs]
            halves = []
            for half in range(2):
                s = s_heads[2 * pair + half]
                m = jnp.max(s, axis=-1, keepdims=True)
                p = jnp.exp(s - m)
                l = jnp.sum(p, axis=-1, keepdims=True)
                halves.append(_dot(p.astype(BF16), vp) / l)
                lse_row = jnp.where(lse_group == 2 * pair + half, m + jnp.log(l), lse_row)
            o_ref[0, c, pl.ds(q0, bq), cs] = jnp.where(low_head, halves[0], halves[1]).astype(BF16)
        lse_ref[0, c, pl.ds(q0, bq), :] = lse_row

    nblk = n // bq

    def step(i, carry):
        at = [where_block(i, u) for u in range(unroll)]
        s_heads = scores(*at[0])
        for u in range(unroll):
            s_next = scores(*at[u + 1]) if u + 1 < unroll else None
            finish(*at[u], s_heads)
            s_heads = s_next
        return carry
    lax.fori_loop(0, dil * nblk // unroll, step, 0)


def _dil_attn(q, k, v, *, dil, bq=128):
    batch, _, n, _ = q.shape
    win = min(n, bq + 2 * BAND)
    unroll = ATTN_KEYS_PER_ITER // win
    blk = lambda w: pl.BlockSpec((1, dil, n, w), lambda b: (b, 0, 0, 0))
    return pl.pallas_call(
        functools.partial(_attn_body, dil=dil, n=n, bq=bq, win=win, unroll=unroll),
        grid=(batch,),
        in_specs=[blk(ATT_OUT)] * 3,
        out_specs=[blk(ATT_OUT), blk(LANE)],
        out_shape=[jax.ShapeDtypeStruct((batch, dil, n, ATT_OUT), BF16),
                   jax.ShapeDtypeStruct((batch, dil, n, LANE), F32)],
        compiler_params=_params("parallel"),
        name=f"dil_attn_r{dil}",
    )(q, k, v)


def _mlstm_body(qk_ref, v_ref, og_ref, gc_ref, gr_ref, cw_ref, cb_ref, gain_ref, o_ref,
                pad_ref, qs_ref, kt_ref, hf_ref, hb_ref, cn_ref, m_ref, bcol_ref, brow_ref, gtot_ref,
                *, seq, chunk, row_block=256):
    assert chunk == LANE
    halo = 8
    zeros = jnp.zeros((halo, LANE), F32)
    pad_ref[0:halo, :] = zeros
    pad_ref[seq + halo:seq + 2 * halo, :] = zeros
    nrb = seq // row_block
    for c in range(2 * ML_HEADS):
        cs = slice(c * LANE, (c + 1) * LANE)

        def fill(i, carry):
            r0 = pl.multiple_of(i * row_block, row_block)
            pad_ref[pl.ds(r0 + halo, row_block), :] = qk_ref[0, pl.ds(r0, row_block), cs].astype(F32)
            return carry
        lax.fori_loop(0, nrb, fill, 0)

        def conv(i, carry):
            r0 = pl.multiple_of(i * row_block, row_block)
            acc = jnp.zeros((row_block, LANE), F32) + cb_ref[:, cs]
            for j in range(ML_CONV):
                acc = acc + cw_ref[j:j + 1, cs] * pad_ref[pl.ds(r0 + (halo - ML_CONV // 2 + j), row_block), :]
            y = acc * _sigmoid(acc)
            if c < ML_HEADS:
                qs_ref[pl.ds(r0, row_block), cs] = (y * ML_HEAD_DIM ** -0.5).astype(BF16)
            else:
                yt = y.T
                for sub in range(row_block // chunk):
                    kt_ref[c - ML_HEADS, i * (row_block // chunk) + sub] = (
                        yt[:, sub * chunk:(sub + 1) * chunk].astype(BF16))
            return carry
        lax.fori_loop(0, nrb, conv, 0)

    cn_ref[...] = jnp.zeros_like(cn_ref)
    m_ref[...] = jnp.zeros_like(m_ref)

    nc = seq // chunk
    ri = lax.broadcasted_iota(jnp.int32, (chunk, chunk), 0)
    ci = lax.broadcasted_iota(jnp.int32, (chunk, chunk), 1)
    lower = ri >= ci
    upper = ri <= ci
    lower_f = lower.astype(F32)
    upper_f = upper.astype(F32)

    ones = jnp.ones((chunk, LANE), BF16)

    bwd_lane = lax.broadcasted_iota(jnp.int32, (chunk, LANE), 1) >= 2 * ML_HEADS
    bwd_sublane = lax.broadcasted_iota(jnp.int32, (ML_GATES, chunk), 0) >= 2 * ML_HEADS

    def cumulate(i, carry):
        rows = pl.ds(pl.multiple_of(i * chunk, chunk), chunk)
        lf_col = _log_sigmoid(gc_ref[0, rows, :])
        pre = _dot_exact(lower_f, lf_col)
        bcol_ref[rows, :] = jnp.where(bwd_lane, pre[chunk - 1:chunk, :] - pre + lf_col, pre)
        lf_row = _log_sigmoid(gr_ref[0, :, rows])
        pre = _dot_exact(lf_row, upper_f)
        tot = pre[:, chunk - 1:chunk]
        brow_ref[i] = jnp.where(bwd_sublane, tot - pre + lf_row, pre)
        gtot_ref[i] = jnp.broadcast_to(tot, (ML_GATES, chunk))
        return carry
    lax.fori_loop(0, nc, cumulate, 0)

    def step(it, carry):
        units = []
        for d in range(2):
            cidx = it if d == 0 else nc - 1 - it
            rows = pl.ds(pl.multiple_of(cidx * chunk, chunk), chunk)
            bcol = bcol_ref[rows, :]
            brow = brow_ref[cidx]
            grow = gr_ref[0, :, rows]
            gtot = gtot_ref[cidx]
            for hd in range(ML_HEADS):
                i_gate = (2 * d) * ML_HEADS + hd
                f_gate = (2 * d + 1) * ML_HEADS + hd
                units.append(dict(
                    st=d * ML_HEADS + hd, d=d, hd=hd, cidx=cidx, rows=rows,
                    hs=slice(hd * LANE, (hd + 1) * LANE),
                    b_col=bcol[:, f_gate:f_gate + 1],
                    w_row=brow[f_gate:f_gate + 1, :] - grow[i_gate:i_gate + 1, :],
                    gtot=gtot[f_gate:f_gate + 1, :]))

        for u in units:
            u["q"] = qs_ref[u["rows"], u["hs"]]
            u["kt"] = kt_ref[u["hd"], u["cidx"]]
            u["s"] = _dot(u["q"], u["kt"])
            u["qcn"] = _dot(u["q"], cn_ref[u["st"]].astype(BF16))
        for u in units:
            u["m_row"] = m_ref[u["st"]][0:1, :]
            m_prev = jnp.max(u["m_row"], axis=-1, keepdims=True)
            g = jnp.where(lower if u["d"] == 0 else upper, -u["w_row"], NEG)
            rm = jnp.maximum(jnp.max(g, axis=-1, keepdims=True), m_prev)
            sc = u["s"] * jnp.exp(g - rm)
            u["vext"] = jnp.concatenate([v_ref[0, u["rows"], u["hs"]], ones], axis=1)
            u["nd"] = _dot(sc.astype(BF16), u["vext"])
            u["w_inter"] = jnp.exp(m_prev - rm)
            u["floor"] = jnp.exp(-(u["b_col"] + rm))
        for u in units:
            a_row = u["gtot"] - u["w_row"]
            m_new = jnp.maximum(u["gtot"] + u["m_row"], jnp.max(a_row, axis=-1, keepdims=True))
            decay = jnp.exp(u["gtot"] + u["m_row"] - m_new)
            ktw = (u["kt"].astype(F32) * jnp.exp(a_row - m_new)).astype(BF16)
            cn_ref[u["st"]] = (jnp.concatenate([decay, decay], axis=1) * cn_ref[u["st"]]
                               + _dot(ktw, u["vext"]))
            m_ref[u["st"]] = jnp.broadcast_to(m_new, m_ref.shape[1:])
        for u in units:
            tot = u["nd"] + u["w_inter"] * u["qcn"]
            hout = tot[:, :LANE] / jnp.maximum(jnp.abs(tot[:, LANE:]), u["floor"])
            if u["d"] == 0:
                hf_ref[u["rows"], u["hs"]] = hout
            else:
                hb_ref[u["rows"], u["hs"]] = hout
        return carry

    lax.fori_loop(0, nc, step, 0)

    def finish(i, carry):
        rows = pl.ds(pl.multiple_of(i * row_block, row_block), row_block)
        for hd in range(ML_HEADS):
            hs = slice(hd * LANE, (hd + 1) * LANE)
            hm = _rms(hf_ref[rows, hs] + hb_ref[rows, hs]) * gain_ref[:, hs]
            o_ref[0, rows, hs] = (hm * og_ref[0, rows, hs].astype(F32)).astype(BF16)
        return carry
    lax.fori_loop(0, nrb, finish, 0)


def _mlstm(mqk, mv, og, gates_col, gates_row, conv_w, conv_b, out_gain, *, batch, seq, chunk=ML_CHUNK):
    halo = 8
    b3 = lambda a: a.reshape(batch, seq, a.shape[-1])
    per_b = lambda w: pl.BlockSpec((1, seq, w), lambda b: (b, 0, 0))
    y = pl.pallas_call(
        functools.partial(_mlstm_body, seq=seq, chunk=chunk),
        grid=(batch,),
        in_specs=[per_b(2 * ML_WIDTH), per_b(ML_WIDTH), per_b(ML_WIDTH), per_b(LANE),
                  pl.BlockSpec((1, ML_GATES, seq), lambda b: (b, 0, 0)),
                  _resident(conv_w.shape), _resident(conv_b.shape), _resident(out_gain.shape)],
        out_specs=per_b(ML_WIDTH),
        out_shape=jax.ShapeDtypeStruct((batch, seq, ML_WIDTH), BF16),
        scratch_shapes=[
            pltpu.VMEM((seq + 2 * halo, LANE), F32),
            pltpu.VMEM((seq, ML_WIDTH), BF16),
            pltpu.VMEM((ML_HEADS, seq // chunk, ML_HEAD_DIM, chunk), BF16),
            pltpu.VMEM((seq, ML_WIDTH), F32),
            pltpu.VMEM((seq, ML_WIDTH), F32),
            pltpu.VMEM((2 * ML_HEADS, ML_HEAD_DIM, 2 * LANE), F32),
            pltpu.VMEM((2 * ML_HEADS, 8, LANE), F32),
            pltpu.VMEM((seq, LANE), F32),
            pltpu.VMEM((seq // chunk, ML_GATES, chunk), F32),
            pltpu.VMEM((seq // chunk, ML_GATES, chunk), F32),
        ],
        compiler_params=_params("parallel"),
        name="mlstm",
    )(b3(mqk), b3(mv), b3(og), b3(gates_col), gates_row, conv_w, conv_b, out_gain)
    return y.reshape(batch * seq, ML_WIDTH)


def _mem_kv_body(mem_ref, gn_ref, w_ref, gk_ref, ck_ref, cv_ref):
    h = (_rms(mem_ref[0]) * gn_ref[...]).astype(BF16)
    kv = _dot(h, w_ref[...])
    for hd in range(CX_HEADS):
        cs = slice(hd * CX_HEAD_DIM, (hd + 1) * CX_HEAD_DIM)
        ck_ref[0, :, cs] = (_rms(kv[:, cs]) * gk_ref[...]).astype(BF16)
    cv_ref[0] = kv[:, CX_WIDTH:].astype(BF16)


def _mem_kv(mem, gn, w, gk):
    b, mlen, dm = mem.shape
    blk = lambda w_: pl.BlockSpec((1, mlen, w_), lambda i: (i, 0, 0))
    return pl.pallas_call(
        _mem_kv_body,
        grid=(b,),
        in_specs=[blk(dm), _resident((1, dm)), _resident(w.shape), _resident(gk.shape)],
        out_specs=[blk(CX_WIDTH)] * 2,
        out_shape=[jax.ShapeDtypeStruct((b, mlen, CX_WIDTH), BF16)] * 2,
        compiler_params=_params("parallel"),
        name="mem_kv",
    )(mem, gn, w, gk)


def _merge_body(x_ref, o0_ref, o1_ref, o2_ref, l0_ref, l1_ref, l2_ref, yml_ref, cq_ref, g_ref, ck_ref, cv_ref,
                wa_ref, wm_ref, wc_ref, wout_ref, ex_ref, out_ref, nat_ref):
    tm, dm = x_ref.shape
    nslab = ATT_OUT // LANE

    def natural(ref, dil, width):
        if dil == 1:
            return ref[0, 0].astype(F32)
        m = tm // dil
        for c in range(dil):
            for sl in range(width // LANE):
                nat_ref[sl, pl.ds(c, m, stride=dil), :] = ref[0, c, :, sl * LANE:(sl + 1) * LANE].astype(F32)
        return jnp.concatenate([nat_ref[sl] for sl in range(width // LANE)], axis=1)

    dils = [dil for _, dil in DIL_GROUPS]
    lses = [natural(ref, dil, LANE) for ref, dil in zip((l0_ref, l1_ref, l2_ref), dils)]
    top = jnp.maximum(jnp.maximum(lses[0], lses[1]), lses[2])
    es = [jnp.exp(l - top) for l in lses]
    inv = 1.0 / (es[0] + es[1] + es[2])
    ex = ex_ref[...]
    y_att = jnp.zeros((tm, ATT_OUT), F32)
    for e, o_ref, dil in zip(es, (o0_ref, o1_ref, o2_ref), dils):
        alpha = e * inv
        hi = alpha.astype(BF16)
        lo = (alpha - hi.astype(F32)).astype(BF16)
        y_att = y_att + (_dot(hi, ex) + _dot(lo, ex)) * natural(o_ref, dil, nslab * LANE)

    y_cx = []
    for hd in range(CX_HEADS):
        cs = slice(hd * CX_HEAD_DIM, (hd + 1) * CX_HEAD_DIM)
        s = _dot_nt(cq_ref[:, cs], ck_ref[0, :, cs])
        p = jnp.exp(s - jnp.max(s, axis=-1, keepdims=True))
        l = jnp.sum(p, axis=-1, keepdims=True)
        y_cx.append(_dot(p.astype(BF16), cv_ref[0, :, cs]) / l)
    y_cx = jnp.concatenate(y_cx, axis=1)

    merged = (g_ref[:, 0:dm].astype(F32) * _dot(y_att.astype(BF16), wa_ref[...])
              + g_ref[:, dm:2 * dm].astype(F32) * _dot(yml_ref[...], wm_ref[...])
              + g_ref[:, 2 * dm:3 * dm].astype(F32) * _dot(y_cx.astype(BF16), wc_ref[...]))
    out_ref[...] = x_ref[...] + _dot(merged.astype(BF16), wout_ref[...])


def _merge(x, o_att, lse_att, y_ml, cq, gates, ck, cv, wa, wm, wc, wout, expand, *, seq, tm=256):
    t, dm = x.shape
    mlen = ck.shape[1]
    nseq = seq // tm
    tok = lambda w: pl.BlockSpec((tm, w), lambda i: (i, 0))
    mem = pl.BlockSpec((1, mlen, CX_WIDTH), lambda i: (i // nseq, 0, 0))
    res = lambda dil, w: pl.BlockSpec((1, dil, tm // dil, w), lambda i: (i // nseq, 0, i % nseq, 0))
    dils = [dil for _, dil in DIL_GROUPS]
    return pl.pallas_call(
        _merge_body,
        grid=(t // tm,),
        in_specs=[tok(dm)] + [res(dil, ATT_OUT) for dil in dils] + [res(dil, LANE) for dil in dils]
                 + [tok(ML_WIDTH), tok(CX_WIDTH), tok(N_BRANCH * dm), mem, mem,
                    _resident(wa.shape), _resident(wm.shape), _resident(wc.shape), _resident(wout.shape),
                    _resident(expand.shape)],
        out_specs=tok(dm),
        out_shape=jax.ShapeDtypeStruct((t, dm), F32),
        scratch_shapes=[pltpu.VMEM((ATT_OUT // LANE, tm, LANE), F32)],
        compiler_params=_params("parallel"),
        name="merge",
    )(x, *o_att, *lse_att, y_ml, cq, gates, ck, cv, wa, wm, wc, wout, expand)


def _rope_tables(seq, tm):
    d = ATT_HEAD_DIM
    inv = ROPE_THETA ** (-jnp.arange(0, d, 2, dtype=F32) / d)
    ang = jnp.arange(seq, dtype=F32)[:, None] * inv[None, :]
    cos, sin = jnp.cos(ang), jnp.sin(ang)
    cos_t = jnp.tile(jnp.concatenate([cos, cos], axis=-1), (1, ATT_SLOTS))
    sin_t = jnp.tile(jnp.concatenate([-sin, sin], axis=-1), (1, ATT_SLOTS))
    tables = []
    for _, dil in DIL_GROUPS:
        perm = lambda a: a.reshape(seq // tm, tm // dil, dil, -1).swapaxes(1, 2).reshape(seq, -1)
        tables += [perm(cos_t), perm(sin_t)]
    return tables


def _layer(x, mem, p):
    batch, seq, dm = x.shape
    xt = x.reshape(batch * seq, dm)
    row = lambda a: a.reshape(1, -1).astype(F32)
    bf = lambda a: a.astype(BF16)

    x1 = _ffn(xt, row(p["norm_ffn1"]), *_ffn_weights(p["w_ffn1_in"], p["w_ffn1_out"]), row(p["norm_final"]),
              final_norm=False)

    w_in = bf(p["w_in"])
    sizes = (ATT_QKV, ATT_QKV, ATT_QKV, ML_WIDTH, ML_WIDTH, ML_WIDTH, ML_WIDTH, ML_GATES, CX_WIDTH, N_BRANCH * dm)
    offs = [0]
    for s in sizes:
        offs.append(offs[-1] + s)
    col = lambda i, j=None: w_in[:, offs[i]:offs[(i if j is None else j) + 1]]
    gn = row(p["norm_mix"])

    idx = jnp.arange(MXU_DIM) // ATT_HEAD_DIM
    e64 = jnp.where(idx[:, None] == idx[None, :], 1.0 / ATT_HEAD_DIM, 0.0).astype(BF16)
    tile8 = lambda a: jnp.tile(row(a), (1, ATT_SLOTS))
    qkv = _qkv_proj(x1, gn, col(0), col(1), col(2), tile8(p["att_q_gain"]), tile8(p["att_k_gain"]),
                    _rope_tables(seq, QKV_TILE), e64, batch=batch, seq=seq, tm=QKV_TILE)

    w_if = jnp.pad(col(7), ((0, 0), (0, LANE - ML_GATES)))
    b_if = jnp.pad(row(p["ml_gate_b"]), ((0, 0), (0, LANE - ML_GATES)))
    mqk, mv, og, mif, cq, gates = _misc_proj(
        x1, gn, col(3, 4), col(5), col(6), w_if, b_if, col(8), row(p["cx_q_gain"]), col(9),
        row(p["mix_gate_b"]))

    o_att, lse_att = [], []
    for g, (_, dil) in enumerate(DIL_GROUPS):
        o, lse = _dil_attn(qkv[g], qkv[3 + g], qkv[6 + g], dil=dil)
        o_att.append(o)
        lse_att.append(lse)

    gates_row = jnp.swapaxes(mif.reshape(batch, seq, LANE)[:, :, :ML_GATES], 1, 2)
    y_ml = _mlstm(mqk, mv, og, mif, gates_row, p["ml_conv_w"].astype(F32), row(p["ml_conv_b"]),
                  row(p["ml_out_gain"]), batch=batch, seq=seq)

    ck, cv = _mem_kv(mem, row(p["norm_mem"]), bf(p["w_mem_kv"]), row(p["cx_k_gain"]))

    lanes = jnp.arange(ATT_OUT) // ATT_HEAD_DIM * LSE_LANES_PER_HEAD
    expand = (jnp.arange(LANE)[:, None] == lanes[None, :]).astype(BF16)
    x2 = _merge(x1, o_att, lse_att, y_ml, cq, gates, ck, cv, bf(p["w_br_att"]), bf(p["w_br_ml"]),
                bf(p["w_br_cx"]), bf(p["w_out"]), expand, seq=seq)

    x3 = _ffn(x2, row(p["norm_ffn2"]), *_ffn_weights(p["w_ffn2_in"], p["w_ffn2_out"]), row(p["norm_final"]),
              final_norm=True)
    return x3.reshape(batch, seq, dm)


def kernel(x, mem, norm_ffn1, w_ffn1_in, w_ffn1_out, norm_mix, norm_mem, w_in, att_q_gain, att_k_gain,
           ml_conv_w, ml_conv_b, ml_gate_b, ml_out_gain, cx_q_gain, cx_k_gain, w_mem_kv, mix_gate_b,
           w_br_att, w_br_ml, w_br_cx, w_out, norm_ffn2, w_ffn2_in, w_ffn2_out, norm_final):
    params = dict(norm_ffn1=norm_ffn1, w_ffn1_in=w_ffn1_in, w_ffn1_out=w_ffn1_out, norm_mix=norm_mix,
                  norm_mem=norm_mem, w_in=w_in, att_q_gain=att_q_gain, att_k_gain=att_k_gain,
                  ml_conv_w=ml_conv_w, ml_conv_b=ml_conv_b, ml_gate_b=ml_gate_b, ml_out_gain=ml_out_gain,
                  cx_q_gain=cx_q_gain, cx_k_gain=cx_k_gain, w_mem_kv=w_mem_kv, mix_gate_b=mix_gate_b,
                  w_br_att=w_br_att, w_br_ml=w_br_ml, w_br_cx=w_br_cx, w_out=w_out, norm_ffn2=norm_ffn2,
                  w_ffn2_in=w_ffn2_in, w_ffn2_out=w_ffn2_out, norm_final=norm_final)
    depth = norm_ffn1.shape[0]
    for layer in range(depth):
        x = _layer(x, mem, {k: v[layer] for k, v in params.items()})
    return x
```

```python
import functools

import jax
import jax.numpy as jnp
from jax import lax
from jax.experimental import pallas as pl
from jax.experimental.pallas import tpu as pltpu

F32 = jnp.float32
BF16 = jnp.bfloat16

EPS = 1e-6
ROPE_THETA = 10000.0
NEG = -1e30

ATT_HEAD_DIM = 64
ATT_SLOTS = 8
DIL_GROUPS = ((128, 1), (512, 4), (2048, 16))
ATT_OUT = ATT_SLOTS * ATT_HEAD_DIM
ATT_QKV = len(DIL_GROUPS) * ATT_OUT
ML_HEADS = 4
ML_HEAD_DIM = 128
ML_WIDTH = ML_HEADS * ML_HEAD_DIM
ML_GATES = 4 * ML_HEADS
ML_CHUNK = 128
ML_CONV = 5
CX_HEADS = 4
CX_HEAD_DIM = 128
CX_WIDTH = CX_HEADS * CX_HEAD_DIM
N_BRANCH = 3

LANE = 128
MXU_DIM = 256
VMEM_LIMIT = 56 * 1024 * 1024

LSE_LANES_PER_HEAD = LANE // ATT_SLOTS
QKV_TILE = 512
ATTN_KEYS_PER_ITER = 512
BAND = 64
assert all((w // 2) // r == BAND for w, r in DIL_GROUPS)

NT_DIMS = (((1,), (1,)), ((), ()))
TN_DIMS = (((0,), (0,)), ((), ()))


def _dot(a, b):
    return jnp.dot(a, b, preferred_element_type=F32)


def _dot_nt(a, b):
    return lax.dot_general(a, b, NT_DIMS, preferred_element_type=F32)


def _dot_exact(a, b):
    return jnp.dot(a, b, preferred_element_type=F32, precision=lax.Precision.HIGHEST)


def _rms(x):
    return x * lax.rsqrt(jnp.mean(x * x, axis=-1, keepdims=True) + EPS)


def _sigmoid(x):
    return 1.0 / (1.0 + jnp.exp(-x))


def _log_sigmoid(x):
    return jnp.minimum(x, 0.0) - jnp.log(1.0 + jnp.exp(-jnp.abs(x)))


def _params(*sem):
    return pltpu.CompilerParams(dimension_semantics=sem, vmem_limit_bytes=VMEM_LIMIT)


def _resident(shape):
    return pl.BlockSpec(shape, lambda *_: (0,) * len(shape), pipeline_mode=pl.Buffered(1))


def _ffn_body(x_ref, g_ref, wg_ref, wu_ref, wo_ref, gf_ref, o_ref, h_ref, a_ref, *, final_norm):
    h_ref[...] = (_rms(x_ref[...]) * g_ref[...]).astype(BF16)
    nj, _, tf = wg_ref.shape
    for j in range(nj):
        h = h_ref[...]
        gate = _dot(h, wg_ref[j])
        up = _dot(h, wu_ref[j])
        a_ref[:, j * tf:(j + 1) * tf] = (gate * _sigmoid(gate) * up).astype(BF16)
    y = x_ref[...] + 0.5 * _dot(a_ref[...], wo_ref[...])
    if final_norm:
        y = _rms(y) * gf_ref[...]
    o_ref[...] = y


def _ffn(x, gain, w_gate, w_up, w_out, final_gain, *, final_norm, tm=512):
    t, dm = x.shape
    d_ff = w_out.shape[0]
    return pl.pallas_call(
        functools.partial(_ffn_body, final_norm=final_norm),
        grid=(t // tm,),
        in_specs=[pl.BlockSpec((tm, dm), lambda i: (i, 0)), _resident((1, dm)), _resident(w_gate.shape),
                  _resident(w_up.shape), _resident(w_out.shape), _resident((1, dm))],
        out_specs=pl.BlockSpec((tm, dm), lambda i: (i, 0)),
        out_shape=jax.ShapeDtypeStruct((t, dm), F32),
        scratch_shapes=[pltpu.VMEM((tm, dm), BF16), pltpu.VMEM((tm, d_ff), BF16)],
        compiler_params=_params("parallel"),
        name="ffn_final" if final_norm else "ffn",
    )(x, gain, w_gate, w_up, w_out, final_gain)


def _ffn_weights(w_in, w_out, tf=MXU_DIM):
    dm, two_ff = w_in.shape
    chunks = w_in.astype(BF16).reshape(dm, 2, two_ff // (2 * tf), tf).transpose(1, 2, 0, 3)
    return chunks[0], chunks[1], w_out.astype(BF16)


def _qkv_body(x_ref, gn_ref, wq_ref, wk_ref, wv_ref, gq_ref, gk_ref, cos0_ref, sin0_ref, cos1_ref, sin1_ref,
              cos2_ref, sin2_ref, e_ref, *rest):
    outs, (hs_ref, hp_ref) = rest[:9], rest[9:]
    q_outs, k_outs, v_outs = outs[0:3], outs[3:6], outs[6:9]
    tables = ((cos0_ref, sin0_ref), (cos1_ref, sin1_ref), (cos2_ref, sin2_ref))
    tm, dm = x_ref.shape
    hn = _rms(x_ref[...]) * gn_ref[...]
    for sl in range(dm // LANE):
        hs_ref[sl] = hn[:, sl * LANE:(sl + 1) * LANE]
    lane = lax.broadcasted_iota(jnp.int32, (tm, ATT_OUT), 1)
    first_half = (lane & (ATT_HEAD_DIM - 1)) < ATT_HEAD_DIM // 2
    e = e_ref[...]
    for g, (_, dil) in enumerate(DIL_GROUPS):
        m = tm // dil
        if dil == 1:
            h = hn.astype(BF16)
        else:
            for c in range(dil):
                for sl in range(dm // LANE):
                    hp_ref[c * m:(c + 1) * m, sl * LANE:(sl + 1) * LANE] = (
                        hs_ref[sl, pl.ds(c, m, stride=dil), :].astype(BF16))
            h = hp_ref[...]
        cos = jnp.concatenate([tables[g][0][...]] * (ATT_OUT // LANE), axis=1)
        sin = jnp.concatenate([tables[g][1][...]] * (ATT_OUT // LANE), axis=1)
        cs = slice(g * ATT_OUT, (g + 1) * ATT_OUT)

        def emit(out, val):
            for c in range(dil):
                out[0, c] = val[c * m:(c + 1) * m, :].astype(BF16)

        for w_ref, gain_ref, scale, out in ((wq_ref, gq_ref, ATT_HEAD_DIM ** -0.5, q_outs[g]),
                                            (wk_ref, gk_ref, 1.0, k_outs[g])):
            t = _dot(h, w_ref[:, cs])
            sq = (t * t).astype(BF16)
            ms = jnp.concatenate([_dot(sq[:, c:c + MXU_DIM], e) for c in range(0, ATT_OUT, MXU_DIM)], axis=1)
            tn = t * lax.rsqrt(ms + EPS) * gain_ref[...]
            partner = jnp.where(first_half,
                                pltpu.roll(tn, ATT_OUT - ATT_HEAD_DIM // 2, 1),
                                pltpu.roll(tn, ATT_HEAD_DIM // 2, 1))
            emit(out, (tn * cos + partner * sin) * scale)
        emit(v_outs[g], _dot(h, wv_ref[:, cs]))


def _qkv_proj(x, gn, wq, wk, wv, gq, gk, tables, e64, *, batch, seq, tm=256):
    t, dm = x.shape
    nseq = seq // tm
    pos = pl.BlockSpec((tm, LANE), lambda i: (i % nseq, 0))
    out_specs, out_shape = [], []
    for _ in range(3):
        for _, dil in DIL_GROUPS:
            out_specs.append(pl.BlockSpec((1, dil, tm // dil, ATT_OUT), lambda i: (i // nseq, 0, i % nseq, 0)))
            out_shape.append(jax.ShapeDtypeStruct((batch, dil, seq // dil, ATT_OUT), BF16))
    return pl.pallas_call(
        _qkv_body,
        grid=(t // tm,),
        in_specs=[pl.BlockSpec((tm, dm), lambda i: (i, 0)), _resident((1, dm)), _resident((dm, ATT_QKV)),
                  _resident((dm, ATT_QKV)), _resident((dm, ATT_QKV)), _resident((1, ATT_OUT)),
                  _resident((1, ATT_OUT))] + [pos] * 6 + [_resident((MXU_DIM, MXU_DIM))],
        out_specs=out_specs,
        out_shape=out_shape,
        scratch_shapes=[pltpu.VMEM((dm // LANE, tm, LANE), F32),
                        pltpu.VMEM((tm, dm), BF16)],
        compiler_params=_params("parallel"),
        name="qkv_proj",
    )(x, gn, wq, wk, wv, gq, gk, *tables, e64)


def _misc_body(x_ref, gn_ref, wqk_ref, wv_ref, wo_ref, wif_ref, bif_ref, wcq_ref, gcq_ref, wg_ref, bg_ref,
               mqk_ref, mv_ref, og_ref, mif_ref, cq_ref, g_ref):
    h = (_rms(x_ref[...]) * gn_ref[...]).astype(BF16)
    mqk_ref[...] = _dot(h, wqk_ref[...]).astype(BF16)
    mv_ref[...] = _dot(h, wv_ref[...]).astype(BF16)
    og_ref[...] = _sigmoid(_dot(h, wo_ref[...])).astype(BF16)
    mif_ref[...] = _dot(h, wif_ref[...]) + bif_ref[...]
    t = _dot(h, wcq_ref[...])
    for hd in range(CX_HEADS):
        cs = slice(hd * CX_HEAD_DIM, (hd + 1) * CX_HEAD_DIM)
        cq_ref[:, cs] = (_rms(t[:, cs]) * gcq_ref[...] * CX_HEAD_DIM ** -0.5).astype(BF16)
    dm = x_ref.shape[1]
    for br in range(N_BRANCH):
        cs = slice(br * dm, (br + 1) * dm)
        g_ref[:, cs] = _sigmoid(_dot(h, wg_ref[:, cs]) + bg_ref[:, cs]).astype(BF16)


def _misc_proj(x, gn, wqk, wv, wo, wif, bif, wcq, gcq, wg, bg, *, tm=512):
    t, dm = x.shape
    tok = lambda w: pl.BlockSpec((tm, w), lambda i: (i, 0))
    widths = (2 * ML_WIDTH, ML_WIDTH, ML_WIDTH, LANE, CX_WIDTH, N_BRANCH * dm)
    dtypes = (BF16, BF16, BF16, F32, BF16, BF16)
    return pl.pallas_call(
        _misc_body,
        grid=(t // tm,),
        in_specs=[tok(dm), _resident((1, dm)), _resident(wqk.shape), _resident(wv.shape), _resident(wo.shape),
                  _resident(wif.shape), _resident(bif.shape), _resident(wcq.shape), _resident(gcq.shape),
                  _resident(wg.shape), _resident(bg.shape)],
        out_specs=[tok(w) for w in widths],
        out_shape=[jax.ShapeDtypeStruct((t, w), d) for w, d in zip(widths, dtypes)],
        compiler_params=_params("parallel"),
        name="misc_proj",
    )(x, gn, wqk, wv, wo, wif, bif, wcq, gcq, wg, bg)


def _attn_body(q_ref, k_ref, v_ref, o_ref, lse_ref, *, dil, n, bq, win, unroll):
    lane = lax.broadcasted_iota(jnp.int32, (bq, LANE), 1)
    low_head = lane < ATT_HEAD_DIM
    lse_group = lane // LSE_LANES_PER_HEAD
    qi = lax.broadcasted_iota(jnp.int32, (bq, win), 0)
    ki = lax.broadcasted_iota(jnp.int32, (bq, win), 1)
    ones = jnp.ones((win, LANE), BF16)

    def where_block(i, u):
        blk = i * unroll + u
        c = blk // nblk
        q0 = pl.multiple_of((blk % nblk) * bq, bq)
        ks = pl.multiple_of(jnp.clip(q0 - BAND, 0, n - win), BAND)
        return c, q0, ks

    def scores(c, q0, ks):
        valid = jnp.abs((q0 + qi) - (ks + ki)) <= BAND
        out = []
        for pair in range(ATT_SLOTS // 2):
            cs = slice(pair * LANE, (pair + 1) * LANE)
            qp = q_ref[0, c, pl.ds(q0, bq), cs]
            kp = k_ref[0, c, pl.ds(ks, win), cs]
            for half in range(2):
                qm = jnp.where(low_head if half == 0 else ~low_head, qp, jnp.zeros_like(qp))
                out.append(jnp.where(valid, _dot_nt(qm, kp), NEG))
        return out

    def finish(c, q0, ks, s_heads):
        lse_row = jnp.zeros((bq, LANE), F32)
        for pair in range(ATT_SLOTS // 2):
            cs = slice(pair * LANE, (pair + 1) * LANE)
            vext = jnp.concatenate([v_ref[0, c, pl.ds(ks, win), cs], ones], axis=1)
            halves = []
            for half in range(2):
                s = s_heads[2 * pair + half]
                m = jnp.max(s, axis=-1, keepdims=True)
                p = jnp.exp((s - m).astype(BF16))
                pv = _dot(p, vext)
                l = pv[:, LANE:]
                halves.append(pv[:, :LANE] / l)
                lse_row = jnp.where(lse_group == 2 * pair + half, m + jnp.log(l), lse_row)
            o_ref[0, c, pl.ds(q0, bq), cs] = jnp.where(low_head, halves[0], halves[1]).astype(BF16)
        lse_ref[0, c, pl.ds(q0, bq), :] = lse_row

    nblk = n // bq

    def step(i, carry):
        at = [where_block(i, u) for u in range(unroll)]
        s_heads = scores(*at[0])
        for u in range(unroll):
            s_next = scores(*at[u + 1]) if u + 1 < unroll else None
            finish(*at[u], s_heads)
            s_heads = s_next
        return carry
    lax.fori_loop(0, dil * nblk // unroll, step, 0)


def _dil_attn(q, k, v, *, dil, bq=128):
    batch, _, n, _ = q.shape
    bq = min(bq, n)
    win = min(n, bq + 2 * BAND)
    unroll = max(1, ATTN_KEYS_PER_ITER // win)
    blk = lambda w: pl.BlockSpec((1, dil, n, w), lambda b: (b, 0, 0, 0))
    return pl.pallas_call(
        functools.partial(_attn_body, dil=dil, n=n, bq=bq, win=win, unroll=unroll),
        grid=(batch,),
        in_specs=[blk(ATT_OUT)] * 3,
        out_specs=[blk(ATT_OUT), blk(LANE)],
        out_shape=[jax.ShapeDtypeStruct((batch, dil, n, ATT_OUT), BF16),
                   jax.ShapeDtypeStruct((batch, dil, n, LANE), F32)],
        compiler_params=_params("parallel"),
        name=f"dil_attn_r{dil}",
    )(q, k, v)


def _mlstm_body(qk_ref, v_ref, og_ref, gc_ref, gr_ref, cw_ref, cb_ref, gain_ref, o_ref,
                pad_ref, qs_ref, kt_ref, hf_ref, hb_ref, cn_ref, m_ref, bcol_ref, brow_ref, gtot_ref,
                *, seq, chunk, row_block=256):
    assert chunk == LANE
    halo = 8
    zeros = jnp.zeros((halo, LANE), F32)
    pad_ref[0:halo, :] = zeros
    pad_ref[seq + halo:seq + 2 * halo, :] = zeros
    nrb = seq // row_block
    for c in range(2 * ML_HEADS):
        cs = slice(c * LANE, (c + 1) * LANE)

        def fill(i, carry):
            r0 = pl.multiple_of(i * row_block, row_block)
            pad_ref[pl.ds(r0 + halo, row_block), :] = qk_ref[0, pl.ds(r0, row_block), cs].astype(F32)
            return carry
        lax.fori_loop(0, nrb, fill, 0)

        def conv(i, carry):
            r0 = pl.multiple_of(i * row_block, row_block)
            acc = jnp.zeros((row_block, LANE), F32) + cb_ref[:, cs]
            for j in range(ML_CONV):
                acc = acc + cw_ref[j:j + 1, cs] * pad_ref[pl.ds(r0 + (halo - ML_CONV // 2 + j), row_block), :]
            y = acc * _sigmoid(acc)
            if c < ML_HEADS:
                qs_ref[pl.ds(r0, row_block), cs] = (y * ML_HEAD_DIM ** -0.5).astype(BF16)
            else:
                yt = y.T
                for sub in range(row_block // chunk):
                    kt_ref[c - ML_HEADS, i * (row_block // chunk) + sub] = (
                        yt[:, sub * chunk:(sub + 1) * chunk].astype(BF16))
            return carry
        lax.fori_loop(0, nrb, conv, 0)

    cn_ref[...] = jnp.zeros_like(cn_ref)
    m_ref[...] = jnp.zeros_like(m_ref)

    nc = seq // chunk
    ri = lax.broadcasted_iota(jnp.int32, (chunk, chunk), 0)
    ci = lax.broadcasted_iota(jnp.int32, (chunk, chunk), 1)
    lower = ri >= ci
    upper = ri <= ci
    lower_f = lower.astype(F32)
    upper_f = upper.astype(F32)

    ones = jnp.ones((chunk, LANE), BF16)

    bwd_lane = lax.broadcasted_iota(jnp.int32, (chunk, LANE), 1) >= 2 * ML_HEADS
    bwd_sublane = lax.broadcasted_iota(jnp.int32, (ML_GATES, chunk), 0) >= 2 * ML_HEADS

    def cumulate(i, carry):
        rows = pl.ds(pl.multiple_of(i * chunk, chunk), chunk)
        lf_col = _log_sigmoid(gc_ref[0, rows, :])
        pre = _dot_exact(lower_f, lf_col)
        bcol_ref[rows, :] = jnp.where(bwd_lane, pre[chunk - 1:chunk, :] - pre + lf_col, pre)
        lf_row = _log_sigmoid(gr_ref[0, :, rows])
        pre = _dot_exact(lf_row, upper_f)
        tot = pre[:, chunk - 1:chunk]
        brow_ref[i] = jnp.where(bwd_sublane, tot - pre + lf_row, pre)
        gtot_ref[i] = jnp.broadcast_to(tot, (ML_GATES, chunk))
        return carry
    lax.fori_loop(0, nc, cumulate, 0)

    def step(it, carry):
        units = []
        for d in range(2):
            cidx = it if d == 0 else nc - 1 - it
            rows = pl.ds(pl.multiple_of(cidx * chunk, chunk), chunk)
            bcol = bcol_ref[rows, :]
            brow = brow_ref[cidx]
            grow = gr_ref[0, :, rows]
            gtot = gtot_ref[cidx]
            for hd in range(ML_HEADS):
                i_gate = (2 * d) * ML_HEADS + hd
                f_gate = (2 * d + 1) * ML_HEADS + hd
                units.append(dict(
                    st=d * ML_HEADS + hd, d=d, hd=hd, cidx=cidx, rows=rows,
                    hs=slice(hd * LANE, (hd + 1) * LANE),
                    b_col=bcol[:, f_gate:f_gate + 1],
                    w_row=brow[f_gate:f_gate + 1, :] - grow[i_gate:i_gate + 1, :],
                    gtot=gtot[f_gate:f_gate + 1, :]))

        for u in units:
            u["q"] = qs_ref[u["rows"], u["hs"]]
            u["kt"] = kt_ref[u["hd"], u["cidx"]]
            u["s"] = _dot(u["q"], u["kt"])
            u["qcn"] = _dot(u["q"], cn_ref[u["st"]].astype(BF16))
        for u in units:
            u["m_row"] = m_ref[u["st"]][0:1, :]
            m_prev = jnp.max(u["m_row"], axis=-1, keepdims=True)
            g = jnp.where(lower if u["d"] == 0 else upper, -u["w_row"], NEG)
            rm = jnp.maximum(jnp.max(g, axis=-1, keepdims=True), m_prev)
            sc = u["s"] * jnp.exp(g - rm)
            u["vext"] = jnp.concatenate([v_ref[0, u["rows"], u["hs"]], ones], axis=1)
            u["nd"] = _dot(sc.astype(BF16), u["vext"])
            u["w_inter"] = jnp.exp(m_prev - rm)
            u["floor"] = jnp.exp(-(u["b_col"] + rm))
        for u in units:
            a_row = u["gtot"] - u["w_row"]
            m_new = jnp.maximum(u["gtot"] + u["m_row"], jnp.max(a_row, axis=-1, keepdims=True))
            decay = jnp.exp(u["gtot"] + u["m_row"] - m_new)
            ktw = (u["kt"].astype(F32) * jnp.exp(a_row - m_new)).astype(BF16)
            cn_ref[u["st"]] = (jnp.concatenate([decay, decay], axis=1) * cn_ref[u["st"]]
                               + _dot(ktw, u["vext"]))
            m_ref[u["st"]] = jnp.broadcast_to(m_new, m_ref.shape[1:])
        for u in units:
            tot = u["nd"] + u["w_inter"] * u["qcn"]
            hout = tot[:, :LANE] / jnp.maximum(jnp.abs(tot[:, LANE:]), u["floor"])
            if u["d"] == 0:
                hf_ref[u["rows"], u["hs"]] = hout
            else:
                hb_ref[u["rows"], u["hs"]] = hout
        return carry

    lax.fori_loop(0, nc, step, 0)

    def finish(i, carry):
        rows = pl.ds(pl.multiple_of(i * row_block, row_block), row_block)
        for hd in range(ML_HEADS):
            hs = slice(hd * LANE, (hd + 1) * LANE)
            hm = _rms(hf_ref[rows, hs] + hb_ref[rows, hs]) * gain_ref[:, hs]
            o_ref[0, rows, hs] = (hm * og_ref[0, rows, hs].astype(F32)).astype(BF16)
        return carry
    lax.fori_loop(0, nrb, finish, 0)


def _mlstm(mqk, mv, og, gates_col, gates_row, conv_w, conv_b, out_gain, *, batch, seq, chunk=ML_CHUNK):
    halo = 8
    b3 = lambda a: a.reshape(batch, seq, a.shape[-1])
    per_b = lambda w: pl.BlockSpec((1, seq, w), lambda b: (b, 0, 0))
    y = pl.pallas_call(
        functools.partial(_mlstm_body, seq=seq, chunk=chunk),
        grid=(batch,),
        in_specs=[per_b(2 * ML_WIDTH), per_b(ML_WIDTH), per_b(ML_WIDTH), per_b(LANE),
                  pl.BlockSpec((1, ML_GATES, seq), lambda b: (b, 0, 0)),
                  _resident(conv_w.shape), _resident(conv_b.shape), _resident(out_gain.shape)],
        out_specs=per_b(ML_WIDTH),
        out_shape=jax.ShapeDtypeStruct((batch, seq, ML_WIDTH), BF16),
        scratch_shapes=[
            pltpu.VMEM((seq + 2 * halo, LANE), F32),
            pltpu.VMEM((seq, ML_WIDTH), BF16),
            pltpu.VMEM((ML_HEADS, seq // chunk, ML_HEAD_DIM, chunk), BF16),
            pltpu.VMEM((seq, ML_WIDTH), F32),
            pltpu.VMEM((seq, ML_WIDTH), F32),
            pltpu.VMEM((2 * ML_HEADS, ML_HEAD_DIM, 2 * LANE), F32),
            pltpu.VMEM((2 * ML_HEADS, 8, LANE), F32),
            pltpu.VMEM((seq, LANE), F32),
            pltpu.VMEM((seq // chunk, ML_GATES, chunk), F32),
            pltpu.VMEM((seq // chunk, ML_GATES, chunk), F32),
        ],
        compiler_params=_params("parallel"),
        name="mlstm",
    )(b3(mqk), b3(mv), b3(og), b3(gates_col), gates_row, conv_w, conv_b, out_gain)
    return y.reshape(batch * seq, ML_WIDTH)


def _mem_kv_body(mem_ref, gn_ref, w_ref, gk_ref, ck_ref, cv_ref):
    h = (_rms(mem_ref[0]) * gn_ref[...]).astype(BF16)
    kv = _dot(h, w_ref[...])
    for hd in range(CX_HEADS):
        cs = slice(hd * CX_HEAD_DIM, (hd + 1) * CX_HEAD_DIM)
        ck_ref[0, :, cs] = (_rms(kv[:, cs]) * gk_ref[...]).astype(BF16)
    cv_ref[0] = kv[:, CX_WIDTH:].astype(BF16)


def _mem_kv(mem, gn, w, gk):
    b, mlen, dm = mem.shape
    blk = lambda w_: pl.BlockSpec((1, mlen, w_), lambda i: (i, 0, 0))
    return pl.pallas_call(
        _mem_kv_body,
        grid=(b,),
        in_specs=[blk(dm), _resident((1, dm)), _resident(w.shape), _resident(gk.shape)],
        out_specs=[blk(CX_WIDTH)] * 2,
        out_shape=[jax.ShapeDtypeStruct((b, mlen, CX_WIDTH), BF16)] * 2,
        compiler_params=_params("parallel"),
        name="mem_kv",
    )(mem, gn, w, gk)


def _merge_body(x_ref, o0_ref, o1_ref, o2_ref, l0_ref, l1_ref, l2_ref, yml_ref, cq_ref, g_ref, ck_ref, cv_ref,
                wa_ref, wm_ref, wc_ref, wout_ref, ex_ref, out_ref, nat_ref):
    tm, dm = x_ref.shape
    nslab = ATT_OUT // LANE

    def natural(ref, dil, width):
        if dil == 1:
            return ref[0, 0].astype(F32)
        m = tm // dil
        for c in range(dil):
            for sl in range(width // LANE):
                nat_ref[sl, pl.ds(c, m, stride=dil), :] = ref[0, c, :, sl * LANE:(sl + 1) * LANE].astype(F32)
        return jnp.concatenate([nat_ref[sl] for sl in range(width // LANE)], axis=1)

    dils = [dil for _, dil in DIL_GROUPS]
    lses = [natural(ref, dil, LANE) for ref, dil in zip((l0_ref, l1_ref, l2_ref), dils)]
    top = jnp.maximum(jnp.maximum(lses[0], lses[1]), lses[2])
    es = [jnp.exp(l - top) for l in lses]
    inv = 1.0 / (es[0] + es[1] + es[2])
    ex = ex_ref[...]
    y_att = jnp.zeros((tm, ATT_OUT), F32)
    for e, o_ref, dil in zip(es, (o0_ref, o1_ref, o2_ref), dils):
        alpha = e * inv
        hi = alpha.astype(BF16)
        lo = (alpha - hi.astype(F32)).astype(BF16)
        y_att = y_att + (_dot(hi, ex) + _dot(lo, ex)) * natural(o_ref, dil, nslab * LANE)

    y_cx = []
    for hd in range(CX_HEADS):
        cs = slice(hd * CX_HEAD_DIM, (hd + 1) * CX_HEAD_DIM)
        s = _dot_nt(cq_ref[:, cs], ck_ref[0, :, cs])
        p = jnp.exp(s - jnp.max(s, axis=-1, keepdims=True))
        l = jnp.sum(p, axis=-1, keepdims=True)
        y_cx.append(_dot(p.astype(BF16), cv_ref[0, :, cs]) / l)
    y_cx = jnp.concatenate(y_cx, axis=1)

    merged = (g_ref[:, 0:dm].astype(F32) * _dot(y_att.astype(BF16), wa_ref[...])
              + g_ref[:, dm:2 * dm].astype(F32) * _dot(yml_ref[...], wm_ref[...])
              + g_ref[:, 2 * dm:3 * dm].astype(F32) * _dot(y_cx.astype(BF16), wc_ref[...]))
    out_ref[...] = x_ref[...] + _dot(merged.astype(BF16), wout_ref[...])


def _merge(x, o_att, lse_att, y_ml, cq, gates, ck, cv, wa, wm, wc, wout, expand, *, seq, tm=256):
    t, dm = x.shape
    mlen = ck.shape[1]
    nseq = seq // tm
    tok = lambda w: pl.BlockSpec((tm, w), lambda i: (i, 0))
    mem = pl.BlockSpec((1, mlen, CX_WIDTH), lambda i: (i // nseq, 0, 0))
    res = lambda dil, w: pl.BlockSpec((1, dil, tm // dil, w), lambda i: (i // nseq, 0, i % nseq, 0))
    dils = [dil for _, dil in DIL_GROUPS]
    return pl.pallas_call(
        _merge_body,
        grid=(t // tm,),
        in_specs=[tok(dm)] + [res(dil, ATT_OUT) for dil in dils] + [res(dil, LANE) for dil in dils]
                 + [tok(ML_WIDTH), tok(CX_WIDTH), tok(N_BRANCH * dm), mem, mem,
                    _resident(wa.shape), _resident(wm.shape), _resident(wc.shape), _resident(wout.shape),
                    _resident(expand.shape)],
        out_specs=tok(dm),
        out_shape=jax.ShapeDtypeStruct((t, dm), F32),
        scratch_shapes=[pltpu.VMEM((ATT_OUT // LANE, tm, LANE), F32)],
        compiler_params=_params("parallel"),
        name="merge",
    )(x, *o_att, *lse_att, y_ml, cq, gates, ck, cv, wa, wm, wc, wout, expand)


def _rope_tables(seq, tm):
    d = ATT_HEAD_DIM
    inv = ROPE_THETA ** (-jnp.arange(0, d, 2, dtype=F32) / d)
    ang = jnp.arange(seq, dtype=F32)[:, None] * inv[None, :]
    cos, sin = jnp.cos(ang), jnp.sin(ang)
    cos_t = jnp.tile(jnp.concatenate([cos, cos], axis=-1), (1, LANE // d))
    sin_t = jnp.tile(jnp.concatenate([-sin, sin], axis=-1), (1, LANE // d))
    tables = []
    for _, dil in DIL_GROUPS:
        perm = lambda a: a.reshape(seq // tm, tm // dil, dil, -1).swapaxes(1, 2).reshape(seq, -1)
        tables += [perm(cos_t), perm(sin_t)]
    return tables


def _layer(x, mem, p):
    batch, seq, dm = x.shape
    xt = x.reshape(batch * seq, dm)
    row = lambda a: a.reshape(1, -1).astype(F32)
    bf = lambda a: a.astype(BF16)

    x1 = _ffn(xt, row(p["norm_ffn1"]), *_ffn_weights(p["w_ffn1_in"], p["w_ffn1_out"]), row(p["norm_final"]),
              final_norm=False)

    w_in = bf(p["w_in"])
    sizes = (ATT_QKV, ATT_QKV, ATT_QKV, ML_WIDTH, ML_WIDTH, ML_WIDTH, ML_WIDTH, ML_GATES, CX_WIDTH, N_BRANCH * dm)
    offs = [0]
    for s in sizes:
        offs.append(offs[-1] + s)
    col = lambda i, j=None: w_in[:, offs[i]:offs[(i if j is None else j) + 1]]
    gn = row(p["norm_mix"])

    idx = jnp.arange(MXU_DIM) // ATT_HEAD_DIM
    e64 = jnp.where(idx[:, None] == idx[None, :], 1.0 / ATT_HEAD_DIM, 0.0).astype(BF16)
    tile8 = lambda a: jnp.tile(row(a), (1, ATT_SLOTS))
    qkv = _qkv_proj(x1, gn, col(0), col(1), col(2), tile8(p["att_q_gain"]), tile8(p["att_k_gain"]),
                    _rope_tables(seq, QKV_TILE), e64, batch=batch, seq=seq, tm=QKV_TILE)

    w_if = jnp.pad(col(7), ((0, 0), (0, LANE - ML_GATES)))
    b_if = jnp.pad(row(p["ml_gate_b"]), ((0, 0), (0, LANE - ML_GATES)))
    mqk, mv, og, mif, cq, gates = _misc_proj(
        x1, gn, col(3, 4), col(5), col(6), w_if, b_if, col(8), row(p["cx_q_gain"]), col(9),
        row(p["mix_gate_b"]))

    o_att, lse_att = [], []
    for g, (_, dil) in enumerate(DIL_GROUPS):
        o, lse = _dil_attn(qkv[g], qkv[3 + g], qkv[6 + g], dil=dil)
        o_att.append(o)
        lse_att.append(lse)

    gates_row = jnp.swapaxes(mif.reshape(batch, seq, LANE)[:, :, :ML_GATES], 1, 2)
    y_ml = _mlstm(mqk, mv, og, mif, gates_row, p["ml_conv_w"].astype(F32), row(p["ml_conv_b"]),
                  row(p["ml_out_gain"]), batch=batch, seq=seq)

    ck, cv = _mem_kv(mem, row(p["norm_mem"]), bf(p["w_mem_kv"]), row(p["cx_k_gain"]))

    lanes = jnp.arange(ATT_OUT) // ATT_HEAD_DIM * LSE_LANES_PER_HEAD
    expand = (jnp.arange(LANE)[:, None] == lanes[None, :]).astype(BF16)
    x2 = _merge(x1, o_att, lse_att, y_ml, cq, gates, ck, cv, bf(p["w_br_att"]), bf(p["w_br_ml"]),
                bf(p["w_br_cx"]), bf(p["w_out"]), expand, seq=seq)

    x3 = _ffn(x2, row(p["norm_ffn2"]), *_ffn_weights(p["w_ffn2_in"], p["w_ffn2_out"]), row(p["norm_final"]),
              final_norm=True)
    return x3.reshape(batch, seq, dm)


def kernel(x, mem, norm_ffn1, w_ffn1_in, w_ffn1_out, norm_mix, norm_mem, w_in, att_q_gain, att_k_gain,
           ml_conv_w, ml_conv_b, ml_gate_b, ml_out_gain, cx_q_gain, cx_k_gain, w_mem_kv, mix_gate_b,
           w_br_att, w_br_ml, w_br_cx, w_out, norm_ffn2, w_ffn2_in, w_ffn2_out, norm_final):
    params = dict(norm_ffn1=norm_ffn1, w_ffn1_in=w_ffn1_in, w_ffn1_out=w_ffn1_out, norm_mix=norm_mix,
                  norm_mem=norm_mem, w_in=w_in, att_q_gain=att_q_gain, att_k_gain=att_k_gain,
                  ml_conv_w=ml_conv_w, ml_conv_b=ml_conv_b, ml_gate_b=ml_gate_b, ml_out_gain=ml_out_gain,
                  cx_q_gain=cx_q_gain, cx_k_gain=cx_k_gain, w_mem_kv=w_mem_kv, mix_gate_b=mix_gate_b,
                  w_br_att=w_br_att, w_br_ml=w_br_ml, w_br_cx=w_br_cx, w_out=w_out, norm_ffn2=norm_ffn2,
                  w_ffn2_in=w_ffn2_in, w_ffn2_out=w_ffn2_out, norm_final=norm_final)
    depth = norm_ffn1.shape[0]
    for layer in range(depth):
        x = _layer(x, mem, {k: v[layer] for k, v in params.items()})
    return x
```

```python
import functools

import jax
import jax.numpy as jnp
from jax import lax
from jax.experimental import pallas as pl
from jax.experimental.pallas import tpu as pltpu

F32 = jnp.float32
BF16 = jnp.bfloat16

EPS = 1e-6
ROPE_THETA = 10000.0
NEG = -1e30

ATT_HEAD_DIM = 64
ATT_SLOTS = 8
DIL_GROUPS = ((128, 1), (512, 4), (2048, 16))
ATT_OUT = ATT_SLOTS * ATT_HEAD_DIM
ATT_QKV = len(DIL_GROUPS) * ATT_OUT
ML_HEADS = 4
ML_HEAD_DIM = 128
ML_WIDTH = ML_HEADS * ML_HEAD_DIM
ML_GATES = 4 * ML_HEADS
ML_CHUNK = 128
ML_CONV = 5
CX_HEADS = 4
CX_HEAD_DIM = 128
CX_WIDTH = CX_HEADS * CX_HEAD_DIM
N_BRANCH = 3

LANE = 128
MXU_DIM = 256
VMEM_LIMIT = 56 * 1024 * 1024

LSE_LANES_PER_HEAD = LANE // ATT_SLOTS
QKV_TILE = 512
ATTN_KEYS_PER_ITER = 512
BAND = 64
assert all((w // 2) // r == BAND for w, r in DIL_GROUPS)

NT_DIMS = (((1,), (1,)), ((), ()))
TN_DIMS = (((0,), (0,)), ((), ()))


def _dot(a, b):
    return jnp.dot(a, b, preferred_element_type=F32)


def _dot_nt(a, b):
    return lax.dot_general(a, b, NT_DIMS, preferred_element_type=F32)


def _dot_exact(a, b):
    return jnp.dot(a, b, preferred_element_type=F32, precision=lax.Precision.HIGHEST)


def _rms(x):
    return x * lax.rsqrt(jnp.mean(x * x, axis=-1, keepdims=True) + EPS)


def _sigmoid(x):
    return 1.0 / (1.0 + jnp.exp(-x))


def _log_sigmoid(x):
    return jnp.minimum(x, 0.0) - jnp.log(1.0 + jnp.exp(-jnp.abs(x)))


def _params(*sem):
    return pltpu.CompilerParams(dimension_semantics=sem, vmem_limit_bytes=VMEM_LIMIT)


def _resident(shape):
    return pl.BlockSpec(shape, lambda *_: (0,) * len(shape), pipeline_mode=pl.Buffered(1))


def _ffn_body(x_ref, g_ref, wi_ref, wo_ref, gf_ref, o_ref, h_ref, a_ref, *, final_norm, tf):
    h_ref[...] = (_rms(x_ref[...]) * g_ref[...]).astype(BF16)
    d_ff = wo_ref.shape[0]
    for j in range(0, d_ff, tf):
        h = h_ref[...]
        gate = _dot(h, wi_ref[:, j:j + tf])
        up = _dot(h, wi_ref[:, d_ff + j:d_ff + j + tf])
        a_ref[:, j:j + tf] = (gate * _sigmoid(gate) * up).astype(BF16)
    y = x_ref[...] + 0.5 * _dot(a_ref[...], wo_ref[...])
    if final_norm:
        y = _rms(y) * gf_ref[...]
    o_ref[...] = y


def _ffn(x, gain, w_in, w_out, final_gain, *, final_norm, tm=512, tf=MXU_DIM):
    t, dm = x.shape
    d_ff = w_out.shape[0]
    return pl.pallas_call(
        functools.partial(_ffn_body, final_norm=final_norm, tf=tf),
        grid=(t // tm,),
        in_specs=[pl.BlockSpec((tm, dm), lambda i: (i, 0)), _resident((1, dm)), _resident(w_in.shape),
                  _resident(w_out.shape), _resident((1, dm))],
        out_specs=pl.BlockSpec((tm, dm), lambda i: (i, 0)),
        out_shape=jax.ShapeDtypeStruct((t, dm), F32),
        scratch_shapes=[pltpu.VMEM((tm, dm), BF16), pltpu.VMEM((tm, d_ff), BF16)],
        compiler_params=_params("parallel"),
        name="ffn_final" if final_norm else "ffn",
    )(x, gain, w_in, w_out, final_gain)


def _qkv_body(x_ref, gn_ref, wq_ref, wk_ref, wv_ref, gq_ref, gk_ref, cos0_ref, sin0_ref, cos1_ref, sin1_ref,
              cos2_ref, sin2_ref, e_ref, *rest):
    outs, (hs_ref, hp_ref) = rest[:9], rest[9:]
    q_outs, k_outs, v_outs = outs[0:3], outs[3:6], outs[6:9]
    tables = ((cos0_ref, sin0_ref), (cos1_ref, sin1_ref), (cos2_ref, sin2_ref))
    tm, dm = x_ref.shape
    hn = _rms(x_ref[...]) * gn_ref[...]
    for sl in range(dm // LANE):
        hs_ref[sl] = hn[:, sl * LANE:(sl + 1) * LANE]
    lane = lax.broadcasted_iota(jnp.int32, (tm, ATT_OUT), 1)
    first_half = (lane & (ATT_HEAD_DIM - 1)) < ATT_HEAD_DIM // 2
    e = e_ref[...]
    for g, (_, dil) in enumerate(DIL_GROUPS):
        m = tm // dil
        if dil == 1:
            h = hn.astype(BF16)
        else:
            for c in range(dil):
                for sl in range(dm // LANE):
                    hp_ref[c * m:(c + 1) * m, sl * LANE:(sl + 1) * LANE] = (
                        hs_ref[sl, pl.ds(c, m, stride=dil), :].astype(BF16))
            h = hp_ref[...]
        cos = jnp.concatenate([tables[g][0][...]] * (ATT_OUT // LANE), axis=1)
        sin = jnp.concatenate([tables[g][1][...]] * (ATT_OUT // LANE), axis=1)
        cs = slice(g * ATT_OUT, (g + 1) * ATT_OUT)

        def emit(out, val):
            for c in range(dil):
                out[0, c] = val[c * m:(c + 1) * m, :].astype(BF16)

        for w_ref, gain_ref, scale, out in ((wq_ref, gq_ref, ATT_HEAD_DIM ** -0.5, q_outs[g]),
                                            (wk_ref, gk_ref, 1.0, k_outs[g])):
            t = _dot(h, w_ref[:, cs])
            sq = (t * t).astype(BF16)
            ms = jnp.concatenate([_dot(sq[:, c:c + MXU_DIM], e) for c in range(0, ATT_OUT, MXU_DIM)], axis=1)
            tn = t * lax.rsqrt(ms + EPS) * gain_ref[...]
            partner = jnp.where(first_half,
                                pltpu.roll(tn, ATT_OUT - ATT_HEAD_DIM // 2, 1),
                                pltpu.roll(tn, ATT_HEAD_DIM // 2, 1))
            emit(out, (tn * cos + partner * sin) * scale)
        emit(v_outs[g], _dot(h, wv_ref[:, cs]))


def _qkv_proj(x, gn, wq, wk, wv, gq, gk, tables, e64, *, batch, seq, tm=256):
    t, dm = x.shape
    nseq = seq // tm
    pos = pl.BlockSpec((tm, LANE), lambda i: (i % nseq, 0))
    out_specs, out_shape = [], []
    for _ in range(3):
        for _, dil in DIL_GROUPS:
            out_specs.append(pl.BlockSpec((1, dil, tm // dil, ATT_OUT), lambda i: (i // nseq, 0, i % nseq, 0)))
            out_shape.append(jax.ShapeDtypeStruct((batch, dil, seq // dil, ATT_OUT), BF16))
    return pl.pallas_call(
        _qkv_body,
        grid=(t // tm,),
        in_specs=[pl.BlockSpec((tm, dm), lambda i: (i, 0)), _resident((1, dm)), _resident((dm, ATT_QKV)),
                  _resident((dm, ATT_QKV)), _resident((dm, ATT_QKV)), _resident((1, ATT_OUT)),
                  _resident((1, ATT_OUT))] + [pos] * 6 + [_resident((MXU_DIM, MXU_DIM))],
        out_specs=out_specs,
        out_shape=out_shape,
        scratch_shapes=[pltpu.VMEM((dm // LANE, tm, LANE), F32),
                        pltpu.VMEM((tm, dm), BF16)],
        compiler_params=_params("parallel"),
        name="qkv_proj",
    )(x, gn, wq, wk, wv, gq, gk, *tables, e64)


def _misc_body(x_ref, gn_ref, wqk_ref, wv_ref, wo_ref, wif_ref, bif_ref, wcq_ref, gcq_ref, wg_ref, bg_ref,
               mqk_ref, mv_ref, og_ref, mif_ref, mift_ref, cq_ref, g_ref):
    h = (_rms(x_ref[...]) * gn_ref[...]).astype(BF16)
    mqk_ref[...] = _dot(h, wqk_ref[...]).astype(BF16)
    mv_ref[...] = _dot(h, wv_ref[...]).astype(BF16)
    og_ref[...] = _sigmoid(_dot(h, wo_ref[...])).astype(BF16)
    mif = _dot(h, wif_ref[...]) + bif_ref[...]
    mif_ref[...] = mif
    mift_ref[0] = mif.T[:ML_GATES, :]
    t = _dot(h, wcq_ref[...])
    for hd in range(CX_HEADS):
        cs = slice(hd * CX_HEAD_DIM, (hd + 1) * CX_HEAD_DIM)
        cq_ref[:, cs] = (_rms(t[:, cs]) * gcq_ref[...] * CX_HEAD_DIM ** -0.5).astype(BF16)
    dm = x_ref.shape[1]
    for br in range(N_BRANCH):
        cs = slice(br * dm, (br + 1) * dm)
        g_ref[:, cs] = _sigmoid(_dot(h, wg_ref[:, cs]) + bg_ref[:, cs]).astype(BF16)


def _misc_proj(x, gn, wqk, wv, wo, wif, bif, wcq, gcq, wg, bg, *, batch, seq, tm=512):
    t, dm = x.shape
    nseq = seq // tm
    tok = lambda w: pl.BlockSpec((tm, w), lambda i: (i, 0))
    outs = [(tok(2 * ML_WIDTH), (t, 2 * ML_WIDTH), BF16), (tok(ML_WIDTH), (t, ML_WIDTH), BF16),
            (tok(ML_WIDTH), (t, ML_WIDTH), BF16), (tok(LANE), (t, LANE), F32),
            (pl.BlockSpec((1, ML_GATES, tm), lambda i: (i // nseq, 0, i % nseq)), (batch, ML_GATES, seq), F32),
            (tok(CX_WIDTH), (t, CX_WIDTH), BF16), (tok(N_BRANCH * dm), (t, N_BRANCH * dm), BF16)]
    return pl.pallas_call(
        _misc_body,
        grid=(t // tm,),
        in_specs=[tok(dm), _resident((1, dm)), _resident(wqk.shape), _resident(wv.shape), _resident(wo.shape),
                  _resident(wif.shape), _resident(bif.shape), _resident(wcq.shape), _resident(gcq.shape),
                  _resident(wg.shape), _resident(bg.shape)],
        out_specs=[spec for spec, _, _ in outs],
        out_shape=[jax.ShapeDtypeStruct(shape, dtype) for _, shape, dtype in outs],
        compiler_params=_params("parallel"),
        name="misc_proj",
    )(x, gn, wqk, wv, wo, wif, bif, wcq, gcq, wg, bg)


def _attn_body(q_ref, k_ref, v_ref, o_ref, lse_ref, *, dil, n, bq, win, unroll):
    lane = lax.broadcasted_iota(jnp.int32, (bq, LANE), 1)
    low_head = lane < ATT_HEAD_DIM
    lse_group = lane // LSE_LANES_PER_HEAD
    qi = lax.broadcasted_iota(jnp.int32, (bq, win), 0)
    ki = lax.broadcasted_iota(jnp.int32, (bq, win), 1)
    ones = jnp.ones((win, LANE), BF16)

    def where_block(i, u):
        blk = i * unroll + u
        c = blk // nblk
        q0 = pl.multiple_of((blk % nblk) * bq, bq)
        ks = pl.multiple_of(jnp.clip(q0 - BAND, 0, n - win), BAND)
        return c, q0, ks

    def scores(c, q0, ks):
        valid = jnp.abs((q0 + qi) - (ks + ki)) <= BAND
        out = []
        for pair in range(ATT_SLOTS // 2):
            cs = slice(pair * LANE, (pair + 1) * LANE)
            qp = q_ref[0, c, pl.ds(q0, bq), cs]
            kp = k_ref[0, c, pl.ds(ks, win), cs]
            for half in range(2):
                qm = jnp.where(low_head if half == 0 else ~low_head, qp, jnp.zeros_like(qp))
                out.append(jnp.where(valid, _dot_nt(qm, kp), NEG))
        return out

    def finish(c, q0, ks, s_heads):
        lse_row = jnp.zeros((bq, LANE), F32)
        for pair in range(ATT_SLOTS // 2):
            cs = slice(pair * LANE, (pair + 1) * LANE)
            vext = jnp.concatenate([v_ref[0, c, pl.ds(ks, win), cs], ones], axis=1)
            halves = []
            for half in range(2):
                s = s_heads[2 * pair + half]
                m = jnp.max(s, axis=-1, keepdims=True)
                p = jnp.exp((s - m).astype(BF16))
                pv = _dot(p, vext)
                l = pv[:, LANE:]
                halves.append(pv[:, :LANE] / l)
                lse_row = jnp.where(lse_group == 2 * pair + half, m + jnp.log(l), lse_row)
            o_ref[0, c, pl.ds(q0, bq), cs] = jnp.where(low_head, halves[0], halves[1]).astype(BF16)
        lse_ref[0, c, pl.ds(q0, bq), :] = lse_row

    nblk = n // bq

    def step(i, carry):
        at = [where_block(i, u) for u in range(unroll)]
        s_heads = scores(*at[0])
        for u in range(unroll):
            s_next = scores(*at[u + 1]) if u + 1 < unroll else None
            finish(*at[u], s_heads)
            s_heads = s_next
        return carry
    lax.fori_loop(0, dil * nblk // unroll, step, 0)


def _dil_attn(q, k, v, *, dil, bq=128):
    batch, _, n, _ = q.shape
    bq = min(bq, n)
    win = min(n, bq + 2 * BAND)
    unroll = max(1, ATTN_KEYS_PER_ITER // win)
    blk = lambda w: pl.BlockSpec((1, dil, n, w), lambda b: (b, 0, 0, 0))
    return pl.pallas_call(
        functools.partial(_attn_body, dil=dil, n=n, bq=bq, win=win, unroll=unroll),
        grid=(batch,),
        in_specs=[blk(ATT_OUT)] * 3,
        out_specs=[blk(ATT_OUT), blk(LANE)],
        out_shape=[jax.ShapeDtypeStruct((batch, dil, n, ATT_OUT), BF16),
                   jax.ShapeDtypeStruct((batch, dil, n, LANE), F32)],
        compiler_params=_params("parallel"),
        name=f"dil_attn_r{dil}",
    )(q, k, v)


def _mlstm_body(qk_ref, v_ref, og_ref, gc_ref, gr_ref, cw_ref, cb_ref, gain_ref, o_ref,
                pad_ref, qs_ref, kt_ref, hf_ref, hb_ref, cn_ref, m_ref, bcol_ref, brow_ref, gtot_ref,
                *, seq, chunk, row_block=256):
    assert chunk == LANE
    halo = 8
    zeros = jnp.zeros((halo, LANE), F32)
    pad_ref[0:halo, :] = zeros
    pad_ref[seq + halo:seq + 2 * halo, :] = zeros
    nrb = seq // row_block
    for c in range(2 * ML_HEADS):
        cs = slice(c * LANE, (c + 1) * LANE)

        def fill(i, carry):
            r0 = pl.multiple_of(i * row_block, row_block)
            pad_ref[pl.ds(r0 + halo, row_block), :] = qk_ref[0, pl.ds(r0, row_block), cs].astype(F32)
            return carry
        lax.fori_loop(0, nrb, fill, 0)

        def conv(i, carry):
            r0 = pl.multiple_of(i * row_block, row_block)
            acc = jnp.zeros((row_block, LANE), F32) + cb_ref[:, cs]
            for j in range(ML_CONV):
                acc = acc + cw_ref[j:j + 1, cs] * pad_ref[pl.ds(r0 + (halo - ML_CONV // 2 + j), row_block), :]
            y = acc * _sigmoid(acc)
            if c < ML_HEADS:
                qs_ref[pl.ds(r0, row_block), cs] = (y * ML_HEAD_DIM ** -0.5).astype(BF16)
            else:
                yt = y.T
                for sub in range(row_block // chunk):
                    kt_ref[c - ML_HEADS, i * (row_block // chunk) + sub] = (
                        yt[:, sub * chunk:(sub + 1) * chunk].astype(BF16))
            return carry
        lax.fori_loop(0, nrb, conv, 0)

    cn_ref[...] = jnp.zeros_like(cn_ref)
    m_ref[...] = jnp.zeros_like(m_ref)

    nc = seq // chunk
    ri = lax.broadcasted_iota(jnp.int32, (chunk, chunk), 0)
    ci = lax.broadcasted_iota(jnp.int32, (chunk, chunk), 1)
    lower = ri >= ci
    upper = ri <= ci
    lower_f = lower.astype(F32)
    upper_f = upper.astype(F32)

    ones = jnp.ones((chunk, LANE), BF16)

    bwd_lane = lax.broadcasted_iota(jnp.int32, (chunk, LANE), 1) >= 2 * ML_HEADS
    bwd_sublane = lax.broadcasted_iota(jnp.int32, (ML_GATES, chunk), 0) >= 2 * ML_HEADS

    lf_cols = [_log_sigmoid(gc_ref[0, c * chunk:(c + 1) * chunk, :]) for c in range(nc)]
    pre_cols = _dot_exact(lower_f, jnp.concatenate(lf_cols, axis=1))
    for c in range(nc):
        pre = pre_cols[:, c * LANE:(c + 1) * LANE]
        bcol_ref[c * chunk:(c + 1) * chunk, :] = jnp.where(bwd_lane, pre[chunk - 1:chunk, :] - pre + lf_cols[c], pre)
    lf_rows = [_log_sigmoid(gr_ref[0, :, c * chunk:(c + 1) * chunk]) for c in range(nc)]
    pre_rows = _dot_exact(jnp.concatenate(lf_rows, axis=0), upper_f)
    for c in range(nc):
        pre = pre_rows[c * ML_GATES:(c + 1) * ML_GATES, :]
        tot = pre[:, chunk - 1:chunk]
        brow_ref[c] = jnp.where(bwd_sublane, tot - pre + lf_rows[c], pre)
        gtot_ref[c] = jnp.broadcast_to(tot, (ML_GATES, chunk))

    def step(it, carry):
        units = []
        for d in range(2):
            cidx = it if d == 0 else nc - 1 - it
            rows = pl.ds(pl.multiple_of(cidx * chunk, chunk), chunk)
            bcol = bcol_ref[rows, :]
            brow = brow_ref[cidx]
            grow = gr_ref[0, :, rows]
            gtot = gtot_ref[cidx]
            for hd in range(ML_HEADS):
                i_gate = (2 * d) * ML_HEADS + hd
                f_gate = (2 * d + 1) * ML_HEADS + hd
                units.append(dict(
                    st=d * ML_HEADS + hd, d=d, hd=hd, cidx=cidx, rows=rows,
                    hs=slice(hd * LANE, (hd + 1) * LANE),
                    b_col=bcol[:, f_gate:f_gate + 1],
                    w_row=brow[f_gate:f_gate + 1, :] - grow[i_gate:i_gate + 1, :],
                    gtot=gtot[f_gate:f_gate + 1, :]))

        for u in units:
            u["q"] = qs_ref[u["rows"], u["hs"]]
            u["kt"] = kt_ref[u["hd"], u["cidx"]]
            u["s"] = _dot(u["q"], u["kt"])
            u["qcn"] = _dot(u["q"], cn_ref[u["st"]].astype(BF16))
        for u in units:
            u["m_row"] = m_ref[u["st"]][0:1, :]
            m_prev = jnp.max(u["m_row"], axis=-1, keepdims=True)
            g = jnp.where(lower if u["d"] == 0 else upper, -u["w_row"], NEG)
            rm = jnp.maximum(jnp.max(g, axis=-1, keepdims=True), m_prev)
            sc = u["s"] * jnp.exp(g - rm)
            u["vext"] = jnp.concatenate([v_ref[0, u["rows"], u["hs"]], ones], axis=1)
            u["nd"] = _dot(sc.astype(BF16), u["vext"])
            u["w_inter"] = jnp.exp(m_prev - rm)
            u["floor"] = jnp.exp(-(u["b_col"] + rm))
        for u in units:
            a_row = u["gtot"] - u["w_row"]
            m_new = jnp.maximum(u["gtot"] + u["m_row"], jnp.max(a_row, axis=-1, keepdims=True))
            decay = jnp.exp(u["gtot"] + u["m_row"] - m_new)
            ktw = (u["kt"].astype(F32) * jnp.exp(a_row - m_new)).astype(BF16)
            cn_ref[u["st"]] = (jnp.concatenate([decay, decay], axis=1) * cn_ref[u["st"]]
                               + _dot(ktw, u["vext"]))
            m_ref[u["st"]] = jnp.broadcast_to(m_new, m_ref.shape[1:])
        for u in units:
            tot = u["nd"] + u["w_inter"] * u["qcn"]
            hout = tot[:, :LANE] / jnp.maximum(jnp.abs(tot[:, LANE:]), u["floor"])
            if u["d"] == 0:
                hf_ref[u["rows"], u["hs"]] = hout
            else:
                hb_ref[u["rows"], u["hs"]] = hout
        return carry

    lax.fori_loop(0, nc, step, 0)

    def finish(i, carry):
        rows = pl.ds(pl.multiple_of(i * row_block, row_block), row_block)
        for hd in range(ML_HEADS):
            hs = slice(hd * LANE, (hd + 1) * LANE)
            hm = _rms(hf_ref[rows, hs] + hb_ref[rows, hs]) * gain_ref[:, hs]
            o_ref[0, rows, hs] = (hm * og_ref[0, rows, hs].astype(F32)).astype(BF16)
        return carry
    lax.fori_loop(0, nrb, finish, 0)


def _mlstm(mqk, mv, og, gates_col, gates_row, conv_w, conv_b, out_gain, *, batch, seq, chunk=ML_CHUNK):
    halo = 8
    b3 = lambda a: a.reshape(batch, seq, a.shape[-1])
    per_b = lambda w: pl.BlockSpec((1, seq, w), lambda b: (b, 0, 0))
    y = pl.pallas_call(
        functools.partial(_mlstm_body, seq=seq, chunk=chunk),
        grid=(batch,),
        in_specs=[per_b(2 * ML_WIDTH), per_b(ML_WIDTH), per_b(ML_WIDTH), per_b(LANE),
                  pl.BlockSpec((1, ML_GATES, seq), lambda b: (b, 0, 0)),
                  _resident(conv_w.shape), _resident(conv_b.shape), _resident(out_gain.shape)],
        out_specs=per_b(ML_WIDTH),
        out_shape=jax.ShapeDtypeStruct((batch, seq, ML_WIDTH), BF16),
        scratch_shapes=[
            pltpu.VMEM((seq + 2 * halo, LANE), F32),
            pltpu.VMEM((seq, ML_WIDTH), BF16),
            pltpu.VMEM((ML_HEADS, seq // chunk, ML_HEAD_DIM, chunk), BF16),
            pltpu.VMEM((seq, ML_WIDTH), F32),
            pltpu.VMEM((seq, ML_WIDTH), F32),
            pltpu.VMEM((2 * ML_HEADS, ML_HEAD_DIM, 2 * LANE), F32),
            pltpu.VMEM((2 * ML_HEADS, 8, LANE), F32),
            pltpu.VMEM((seq, LANE), F32),
            pltpu.VMEM((seq // chunk, ML_GATES, chunk), F32),
            pltpu.VMEM((seq // chunk, ML_GATES, chunk), F32),
        ],
        compiler_params=_params("parallel"),
        name="mlstm",
    )(b3(mqk), b3(mv), b3(og), b3(gates_col), gates_row, conv_w, conv_b, out_gain)
    return y.reshape(batch * seq, ML_WIDTH)


def _mem_kv_body(mem_ref, gn_ref, w_ref, gk_ref, ck_ref, cv_ref):
    h = (_rms(mem_ref[0]) * gn_ref[...]).astype(BF16)
    kv = _dot(h, w_ref[...])
    for hd in range(CX_HEADS):
        cs = slice(hd * CX_HEAD_DIM, (hd + 1) * CX_HEAD_DIM)
        ck_ref[0, :, cs] = (_rms(kv[:, cs]) * gk_ref[...]).astype(BF16)
    cv_ref[0] = kv[:, CX_WIDTH:].astype(BF16)


def _mem_kv(mem, gn, w, gk):
    b, mlen, dm = mem.shape
    blk = lambda w_: pl.BlockSpec((1, mlen, w_), lambda i: (i, 0, 0))
    return pl.pallas_call(
        _mem_kv_body,
        grid=(b,),
        in_specs=[blk(dm), _resident((1, dm)), _resident(w.shape), _resident(gk.shape)],
        out_specs=[blk(CX_WIDTH)] * 2,
        out_shape=[jax.ShapeDtypeStruct((b, mlen, CX_WIDTH), BF16)] * 2,
        compiler_params=_params("parallel"),
        name="mem_kv",
    )(mem, gn, w, gk)


def _merge_body(x_ref, o0_ref, o1_ref, o2_ref, l0_ref, l1_ref, l2_ref, yml_ref, cq_ref, g_ref, ck_ref, cv_ref,
                wa_ref, wm_ref, wc_ref, wout_ref, ex_ref, out_ref, nat_ref):
    tm, dm = x_ref.shape
    nslab = ATT_OUT // LANE

    def natural(ref, dil, width):
        if dil == 1:
            return ref[0, 0].astype(F32)
        m = tm // dil
        for c in range(dil):
            for sl in range(width // LANE):
                nat_ref[sl, pl.ds(c, m, stride=dil), :] = ref[0, c, :, sl * LANE:(sl + 1) * LANE].astype(F32)
        return jnp.concatenate([nat_ref[sl] for sl in range(width // LANE)], axis=1)

    dils = [dil for _, dil in DIL_GROUPS]
    lses = [natural(ref, dil, LANE) for ref, dil in zip((l0_ref, l1_ref, l2_ref), dils)]
    top = jnp.maximum(jnp.maximum(lses[0], lses[1]), lses[2])
    es = [jnp.exp(l - top) for l in lses]
    inv = 1.0 / (es[0] + es[1] + es[2])
    ex = ex_ref[...]
    y_att = jnp.zeros((tm, ATT_OUT), F32)
    for e, o_ref, dil in zip(es, (o0_ref, o1_ref, o2_ref), dils):
        alpha = e * inv
        hi = alpha.astype(BF16)
        lo = (alpha - hi.astype(F32)).astype(BF16)
        y_att = y_att + (_dot(hi, ex) + _dot(lo, ex)) * natural(o_ref, dil, nslab * LANE)

    y_cx = []
    for hd in range(CX_HEADS):
        cs = slice(hd * CX_HEAD_DIM, (hd + 1) * CX_HEAD_DIM)
        s = _dot_nt(cq_ref[:, cs], ck_ref[0, :, cs])
        p = jnp.exp(s - jnp.max(s, axis=-1, keepdims=True))
        l = jnp.sum(p, axis=-1, keepdims=True)
        y_cx.append(_dot(p.astype(BF16), cv_ref[0, :, cs]) / l)
    y_cx = jnp.concatenate(y_cx, axis=1)

    merged = (g_ref[:, 0:dm].astype(F32) * _dot(y_att.astype(BF16), wa_ref[...])
              + g_ref[:, dm:2 * dm].astype(F32) * _dot(yml_ref[...], wm_ref[...])
              + g_ref[:, 2 * dm:3 * dm].astype(F32) * _dot(y_cx.astype(BF16), wc_ref[...]))
    out_ref[...] = x_ref[...] + _dot(merged.astype(BF16), wout_ref[...])


def _merge(x, o_att, lse_att, y_ml, cq, gates, ck, cv, wa, wm, wc, wout, expand, *, seq, tm=512):
    t, dm = x.shape
    mlen = ck.shape[1]
    nseq = seq // tm
    tok = lambda w: pl.BlockSpec((tm, w), lambda i: (i, 0))
    mem = pl.BlockSpec((1, mlen, CX_WIDTH), lambda i: (i // nseq, 0, 0))
    res = lambda dil, w: pl.BlockSpec((1, dil, tm // dil, w), lambda i: (i // nseq, 0, i % nseq, 0))
    dils = [dil for _, dil in DIL_GROUPS]
    return pl.pallas_call(
        _merge_body,
        grid=(t // tm,),
        in_specs=[tok(dm)] + [res(dil, ATT_OUT) for dil in dils] + [res(dil, LANE) for dil in dils]
                 + [tok(ML_WIDTH), tok(CX_WIDTH), tok(N_BRANCH * dm), mem, mem,
                    _resident(wa.shape), _resident(wm.shape), _resident(wc.shape), _resident(wout.shape),
                    _resident(expand.shape)],
        out_specs=tok(dm),
        out_shape=jax.ShapeDtypeStruct((t, dm), F32),
        scratch_shapes=[pltpu.VMEM((ATT_OUT // LANE, tm, LANE), F32)],
        compiler_params=_params("parallel"),
        name="merge",
    )(x, *o_att, *lse_att, y_ml, cq, gates, ck, cv, wa, wm, wc, wout, expand)


def _rope_tables(seq, tm):
    d = ATT_HEAD_DIM
    inv = ROPE_THETA ** (-jnp.arange(0, d, 2, dtype=F32) / d)
    ang = jnp.arange(seq, dtype=F32)[:, None] * inv[None, :]
    cos, sin = jnp.cos(ang), jnp.sin(ang)
    cos_t = jnp.tile(jnp.concatenate([cos, cos], axis=-1), (1, LANE // d))
    sin_t = jnp.tile(jnp.concatenate([-sin, sin], axis=-1), (1, LANE // d))
    tables = []
    for _, dil in DIL_GROUPS:
        perm = lambda a: a.reshape(seq // tm, tm // dil, dil, -1).swapaxes(1, 2).reshape(seq, -1)
        tables += [perm(cos_t), perm(sin_t)]
    return tables


def _layer(x, mem, p):
    batch, seq, dm = x.shape
    xt = x.reshape(batch * seq, dm)
    row = lambda a: a.reshape(1, -1).astype(F32)
    bf = lambda a: a.astype(BF16)

    x1 = _ffn(xt, row(p["norm_ffn1"]), bf(p["w_ffn1_in"]), bf(p["w_ffn1_out"]), row(p["norm_final"]),
              final_norm=False)

    w_in = bf(p["w_in"])
    sizes = (ATT_QKV, ATT_QKV, ATT_QKV, ML_WIDTH, ML_WIDTH, ML_WIDTH, ML_WIDTH, ML_GATES, CX_WIDTH, N_BRANCH * dm)
    offs = [0]
    for s in sizes:
        offs.append(offs[-1] + s)
    col = lambda i, j=None: w_in[:, offs[i]:offs[(i if j is None else j) + 1]]
    gn = row(p["norm_mix"])

    idx = jnp.arange(MXU_DIM) // ATT_HEAD_DIM
    e64 = jnp.where(idx[:, None] == idx[None, :], 1.0 / ATT_HEAD_DIM, 0.0).astype(BF16)
    tile8 = lambda a: jnp.tile(row(a), (1, ATT_SLOTS))
    qkv = _qkv_proj(x1, gn, col(0), col(1), col(2), tile8(p["att_q_gain"]), tile8(p["att_k_gain"]),
                    _rope_tables(seq, QKV_TILE), e64, batch=batch, seq=seq, tm=QKV_TILE)

    w_if = jnp.pad(col(7), ((0, 0), (0, LANE - ML_GATES)))
    b_if = jnp.pad(row(p["ml_gate_b"]), ((0, 0), (0, LANE - ML_GATES)))
    mqk, mv, og, mif, gates_row, cq, gates = _misc_proj(
        x1, gn, col(3, 4), col(5), col(6), w_if, b_if, col(8), row(p["cx_q_gain"]), col(9),
        row(p["mix_gate_b"]), batch=batch, seq=seq)

    o_att, lse_att = [], []
    for g, (_, dil) in enumerate(DIL_GROUPS):
        o, lse = _dil_attn(qkv[g], qkv[3 + g], qkv[6 + g], dil=dil)
        o_att.append(o)
        lse_att.append(lse)

    y_ml = _mlstm(mqk, mv, og, mif, gates_row, p["ml_conv_w"].astype(F32), row(p["ml_conv_b"]),
                  row(p["ml_out_gain"]), batch=batch, seq=seq)

    ck, cv = _mem_kv(mem, row(p["norm_mem"]), bf(p["w_mem_kv"]), row(p["cx_k_gain"]))

    lanes = jnp.arange(ATT_OUT) // ATT_HEAD_DIM * LSE_LANES_PER_HEAD
    expand = (jnp.arange(LANE)[:, None] == lanes[None, :]).astype(BF16)
    x2 = _merge(x1, o_att, lse_att, y_ml, cq, gates, ck, cv, bf(p["w_br_att"]), bf(p["w_br_ml"]),
                bf(p["w_br_cx"]), bf(p["w_out"]), expand, seq=seq)

    x3 = _ffn(x2, row(p["norm_ffn2"]), bf(p["w_ffn2_in"]), bf(p["w_ffn2_out"]), row(p["norm_final"]),
              final_norm=True)
    return x3.reshape(batch, seq, dm)


def kernel(x, mem, norm_ffn1, w_ffn1_in, w_ffn1_out, norm_mix, norm_mem, w_in, att_q_gain, att_k_gain,
           ml_conv_w, ml_conv_b, ml_gate_b, ml_out_gain, cx_q_gain, cx_k_gain, w_mem_kv, mix_gate_b,
           w_br_att, w_br_ml, w_br_cx, w_out, norm_ffn2, w_ffn2_in, w_ffn2_out, norm_final):
    params = dict(norm_ffn1=norm_ffn1, w_ffn1_in=w_ffn1_in, w_ffn1_out=w_ffn1_out, norm_mix=norm_mix,
                  norm_mem=norm_mem, w_in=w_in, att_q_gain=att_q_gain, att_k_gain=att_k_gain,
                  ml_conv_w=ml_conv_w, ml_conv_b=ml_conv_b, ml_gate_b=ml_gate_b, ml_out_gain=ml_out_gain,
                  cx_q_gain=cx_q_gain, cx_k_gain=cx_k_gain, w_mem_kv=w_mem_kv, mix_gate_b=mix_gate_b,
                  w_br_att=w_br_att, w_br_ml=w_br_ml, w_br_cx=w_br_cx, w_out=w_out, norm_ffn2=norm_ffn2,
                  w_ffn2_in=w_ffn2_in, w_ffn2_out=w_ffn2_out, norm_final=norm_final)
    depth = norm_ffn1.shape[0]
    for layer in range(depth):
        x = _layer(x, mem, {k: v[layer] for k, v in params.items()})
    return x
```

```python
import functools

import jax
import jax.numpy as jnp
from jax import lax
from jax.experimental import pallas as pl
from jax.experimental.pallas import tpu as pltpu

F32 = jnp.float32
BF16 = jnp.bfloat16

EPS = 1e-6
ROPE_THETA = 10000.0
NEG = -1e30

ATT_HEAD_DIM = 64
ATT_SLOTS = 8
DIL_GROUPS = ((128, 1), (512, 4), (2048, 16))
ATT_OUT = ATT_SLOTS * ATT_HEAD_DIM
ATT_QKV = len(DIL_GROUPS) * ATT_OUT
ML_HEADS = 4
ML_HEAD_DIM = 128
ML_WIDTH = ML_HEADS * ML_HEAD_DIM
ML_GATES = 4 * ML_HEADS
ML_CHUNK = 128
ML_CONV = 5
CX_HEADS = 4
CX_HEAD_DIM = 128
CX_WIDTH = CX_HEADS * CX_HEAD_DIM
N_BRANCH = 3

LANE = 128
MXU_DIM = 256
VMEM_LIMIT = 56 * 1024 * 1024

LSE_LANES_PER_HEAD = LANE // ATT_SLOTS
QKV_TILE = 512
ATTN_KEYS_PER_ITER = 512
BAND = 64
assert all((w // 2) // r == BAND for w, r in DIL_GROUPS)

NT_DIMS = (((1,), (1,)), ((), ()))
TN_DIMS = (((0,), (0,)), ((), ()))


def _dot(a, b):
    return jnp.dot(a, b, preferred_element_type=F32)


def _dot_nt(a, b):
    return lax.dot_general(a, b, NT_DIMS, preferred_element_type=F32)


def _dot_exact(a, b):
    return jnp.dot(a, b, preferred_element_type=F32, precision=lax.Precision.HIGHEST)


def _rms(x):
    return x * lax.rsqrt(jnp.mean(x * x, axis=-1, keepdims=True) + EPS)


def _sigmoid(x):
    return 1.0 / (1.0 + jnp.exp(-x))


def _log_sigmoid(x):
    return jnp.minimum(x, 0.0) - jnp.log(1.0 + jnp.exp(-jnp.abs(x)))


def _params(*sem):
    return pltpu.CompilerParams(dimension_semantics=sem, vmem_limit_bytes=VMEM_LIMIT)


def _resident(shape):
    return pl.BlockSpec(shape, lambda *_: (0,) * len(shape), pipeline_mode=pl.Buffered(1))


def _ffn_body(x_ref, g_ref, wi_ref, wo_ref, gf_ref, o_ref, h_ref, a_ref, *, final_norm, tf):
    h_ref[...] = (_rms(x_ref[...]) * g_ref[...]).astype(BF16)
    d_ff = wo_ref.shape[0]
    for j in range(0, d_ff, tf):
        h = h_ref[...]
        gate = _dot(h, wi_ref[:, j:j + tf])
        up = _dot(h, wi_ref[:, d_ff + j:d_ff + j + tf])
        a_ref[:, j:j + tf] = (gate * _sigmoid(gate) * up).astype(BF16)
    y = x_ref[...] + 0.5 * _dot(a_ref[...], wo_ref[...])
    if final_norm:
        y = _rms(y) * gf_ref[...]
    o_ref[...] = y


def _ffn(x, gain, w_in, w_out, final_gain, *, final_norm, tm=512, tf=MXU_DIM):
    t, dm = x.shape
    d_ff = w_out.shape[0]
    return pl.pallas_call(
        functools.partial(_ffn_body, final_norm=final_norm, tf=tf),
        grid=(t // tm,),
        in_specs=[pl.BlockSpec((tm, dm), lambda i: (i, 0)), _resident((1, dm)), _resident(w_in.shape),
                  _resident(w_out.shape), _resident((1, dm))],
        out_specs=pl.BlockSpec((tm, dm), lambda i: (i, 0)),
        out_shape=jax.ShapeDtypeStruct((t, dm), F32),
        scratch_shapes=[pltpu.VMEM((tm, dm), BF16), pltpu.VMEM((tm, d_ff), BF16)],
        compiler_params=_params("parallel"),
        name="ffn_final" if final_norm else "ffn",
    )(x, gain, w_in, w_out, final_gain)


def _qkv_body(x_ref, gn_ref, wq_ref, wk_ref, wv_ref, gq_ref, gk_ref, cos0_ref, sin0_ref, cos1_ref, sin1_ref,
              cos2_ref, sin2_ref, e_ref, *rest):
    outs, (hs_ref, hp_ref) = rest[:9], rest[9:]
    q_outs, k_outs, v_outs = outs[0:3], outs[3:6], outs[6:9]
    tables = ((cos0_ref, sin0_ref), (cos1_ref, sin1_ref), (cos2_ref, sin2_ref))
    tm, dm = x_ref.shape
    hn = _rms(x_ref[...]) * gn_ref[...]
    for sl in range(dm // LANE):
        hs_ref[sl] = hn[:, sl * LANE:(sl + 1) * LANE]
    lane = lax.broadcasted_iota(jnp.int32, (tm, ATT_OUT), 1)
    first_half = (lane & (ATT_HEAD_DIM - 1)) < ATT_HEAD_DIM // 2
    e = e_ref[...]
    for g, (_, dil) in enumerate(DIL_GROUPS):
        m = tm // dil
        if dil == 1:
            h = hn.astype(BF16)
        else:
            for c in range(dil):
                for sl in range(dm // LANE):
                    hp_ref[c * m:(c + 1) * m, sl * LANE:(sl + 1) * LANE] = (
                        hs_ref[sl, pl.ds(c, m, stride=dil), :].astype(BF16))
            h = hp_ref[...]
        cos = jnp.concatenate([tables[g][0][...]] * (ATT_OUT // LANE), axis=1)
        sin = jnp.concatenate([tables[g][1][...]] * (ATT_OUT // LANE), axis=1)
        cs = slice(g * ATT_OUT, (g + 1) * ATT_OUT)

        def emit(out, val):
            for c in range(dil):
                out[0, c] = val[c * m:(c + 1) * m, :].astype(BF16)

        for w_ref, gain_ref, scale, out in ((wq_ref, gq_ref, ATT_HEAD_DIM ** -0.5, q_outs[g]),
                                            (wk_ref, gk_ref, 1.0, k_outs[g])):
            t = _dot(h, w_ref[:, cs])
            sq = (t * t).astype(BF16)
            ms = jnp.concatenate([_dot(sq[:, c:c + MXU_DIM], e) for c in range(0, ATT_OUT, MXU_DIM)], axis=1)
            tn = t * lax.rsqrt(ms + EPS) * gain_ref[...]
            partner = jnp.where(first_half,
                                pltpu.roll(tn, ATT_OUT - ATT_HEAD_DIM // 2, 1),
                                pltpu.roll(tn, ATT_HEAD_DIM // 2, 1))
            emit(out, (tn * cos + partner * sin) * scale)
        emit(v_outs[g], _dot(h, wv_ref[:, cs]))


def _qkv_proj(x, gn, wq, wk, wv, gq, gk, tables, e64, *, batch, seq, tm=256):
    t, dm = x.shape
    nseq = seq // tm
    pos = pl.BlockSpec((tm, LANE), lambda i: (i % nseq, 0))
    out_specs, out_shape = [], []
    for _ in range(3):
        for _, dil in DIL_GROUPS:
            out_specs.append(pl.BlockSpec((1, dil, tm // dil, ATT_OUT), lambda i: (i // nseq, 0, i % nseq, 0)))
            out_shape.append(jax.ShapeDtypeStruct((batch, dil, seq // dil, ATT_OUT), BF16))
    return pl.pallas_call(
        _qkv_body,
        grid=(t // tm,),
        in_specs=[pl.BlockSpec((tm, dm), lambda i: (i, 0)), _resident((1, dm)), _resident((dm, ATT_QKV)),
                  _resident((dm, ATT_QKV)), _resident((dm, ATT_QKV)), _resident((1, ATT_OUT)),
                  _resident((1, ATT_OUT))] + [pos] * 6 + [_resident((MXU_DIM, MXU_DIM))],
        out_specs=out_specs,
        out_shape=out_shape,
        scratch_shapes=[pltpu.VMEM((dm // LANE, tm, LANE), F32),
                        pltpu.VMEM((tm, dm), BF16)],
        compiler_params=_params("parallel"),
        name="qkv_proj",
    )(x, gn, wq, wk, wv, gq, gk, *tables, e64)


def _misc_body(x_ref, xprev_ref, xnext_ref, gn_ref, wqk_ref, wv_ref, wo_ref, wif_ref, bif_ref, wcq_ref, gcq_ref,
               wg_ref, bg_ref, cw_ref, cb_ref,
               mq_ref, mkt_ref, mv_ref, og_ref, mif_ref, mift_ref, cq_ref, g_ref, hext_ref, pad_ref,
               *, nseq, chunk, row_block=256):
    tm, dm = x_ref.shape
    halo = xprev_ref.shape[0]
    norm = lambda ref: (_rms(ref[...]) * gn_ref[...]).astype(BF16)
    h = norm(x_ref)

    hext_ref[0:halo, :] = norm(xprev_ref)
    hext_ref[halo:halo + tm, :] = h
    hext_ref[halo + tm:, :] = norm(xnext_ref)
    pad_ref[...] = _dot(hext_ref[...], wqk_ref[...])
    pos = pl.program_id(0) % nseq
    pad_ref[0:halo, :] = pad_ref[0:halo, :] * (pos > 0).astype(F32)
    pad_ref[halo + tm:, :] = pad_ref[halo + tm:, :] * (pos < nseq - 1).astype(F32)

    def conv(c):
        cs = slice(c * LANE, (c + 1) * LANE)
        for r0 in range(0, tm, row_block):
            acc = jnp.zeros((row_block, LANE), F32) + cb_ref[:, cs]
            for j in range(ML_CONV):
                acc = acc + cw_ref[j:j + 1, cs] * pad_ref[pl.ds(halo + r0 - ML_CONV // 2 + j, row_block), cs]
            y = acc * _sigmoid(acc)
            if c < ML_HEADS:
                mq_ref[r0:r0 + row_block, cs] = (y * ML_HEAD_DIM ** -0.5).astype(BF16)
            else:
                yt = y.T
                for sub in range(0, row_block, chunk):
                    mkt_ref[0, c - ML_HEADS, (r0 + sub) // chunk] = yt[:, sub:sub + chunk].astype(BF16)

    def small_streams():
        mv_ref[...] = _dot(h, wv_ref[...]).astype(BF16)
        og_ref[...] = _sigmoid(_dot(h, wo_ref[...])).astype(BF16)
        mif = _dot(h, wif_ref[...]) + bif_ref[...]
        mif_ref[...] = mif
        mift_ref[0] = mif.T[:ML_GATES, :]

    def cross_q():
        t = _dot(h, wcq_ref[...])
        for hd in range(CX_HEADS):
            cs = slice(hd * CX_HEAD_DIM, (hd + 1) * CX_HEAD_DIM)
            cq_ref[:, cs] = (_rms(t[:, cs]) * gcq_ref[...] * CX_HEAD_DIM ** -0.5).astype(BF16)

    def mix_gate(cs):
        g_ref[:, cs] = _sigmoid(_dot(h, wg_ref[:, cs]) + bg_ref[:, cs]).astype(BF16)

    half = dm // 2
    matmuls = [small_streams, cross_q] + [functools.partial(mix_gate, slice(o, o + half))
                                          for o in range(0, N_BRANCH * dm, half)]
    assert len(matmuls) == 2 * ML_HEADS
    for c, matmul in enumerate(matmuls):
        matmul()
        conv(c)


def _misc_proj(x, gn, wqk, wv, wo, wif, bif, wcq, gcq, wg, bg, conv_w, conv_b, *, batch, seq, tm=512,
               chunk=ML_CHUNK):
    t, dm = x.shape
    nseq = seq // tm
    halo = 16
    per_halo = tm // halo
    tok = lambda w: pl.BlockSpec((tm, w), lambda i: (i, 0))
    prev = pl.BlockSpec((halo, dm), lambda i: (jnp.maximum(i * per_halo - 1, 0), 0))
    nxt = pl.BlockSpec((halo, dm), lambda i: (jnp.minimum((i + 1) * per_halo, t // halo - 1), 0))
    outs = [(tok(ML_WIDTH), (t, ML_WIDTH), BF16),
            (pl.BlockSpec((1, ML_HEADS, tm // chunk, ML_HEAD_DIM, chunk), lambda i: (i // nseq, 0, i % nseq, 0, 0)),
             (batch, ML_HEADS, seq // chunk, ML_HEAD_DIM, chunk), BF16),
            (tok(ML_WIDTH), (t, ML_WIDTH), BF16),
            (tok(ML_WIDTH), (t, ML_WIDTH), BF16), (tok(LANE), (t, LANE), F32),
            (pl.BlockSpec((1, ML_GATES, tm), lambda i: (i // nseq, 0, i % nseq)), (batch, ML_GATES, seq), F32),
            (tok(CX_WIDTH), (t, CX_WIDTH), BF16), (tok(N_BRANCH * dm), (t, N_BRANCH * dm), BF16)]
    return pl.pallas_call(
        functools.partial(_misc_body, nseq=nseq, chunk=chunk),
        grid=(t // tm,),
        in_specs=[tok(dm), prev, nxt, _resident((1, dm)), _resident(wqk.shape), _resident(wv.shape),
                  _resident(wo.shape), _resident(wif.shape), _resident(bif.shape), _resident(wcq.shape),
                  _resident(gcq.shape), _resident(wg.shape), _resident(bg.shape), _resident(conv_w.shape),
                  _resident(conv_b.shape)],
        out_specs=[spec for spec, _, _ in outs],
        out_shape=[jax.ShapeDtypeStruct(shape, dtype) for _, shape, dtype in outs],
        scratch_shapes=[pltpu.VMEM((tm + 2 * halo, dm), BF16),
                        pltpu.VMEM((tm + 2 * halo, 2 * ML_WIDTH), F32)],
        compiler_params=_params("parallel"),
        name="misc_proj",
    )(x, x, x, gn, wqk, wv, wo, wif, bif, wcq, gcq, wg, bg, conv_w, conv_b)


def _attn_body(q_ref, k_ref, v_ref, o_ref, lse_ref, *, dil, n, bq, win, unroll):
    lane = lax.broadcasted_iota(jnp.int32, (bq, LANE), 1)
    low_head = lane < ATT_HEAD_DIM
    lse_group = lane // LSE_LANES_PER_HEAD
    qi = lax.broadcasted_iota(jnp.int32, (bq, win), 0)
    ki = lax.broadcasted_iota(jnp.int32, (bq, win), 1)
    ones = jnp.ones((win, LANE), BF16)

    def where_block(i, u):
        blk = i * unroll + u
        c = blk // nblk
        q0 = pl.multiple_of((blk % nblk) * bq, bq)
        ks = pl.multiple_of(jnp.clip(q0 - BAND, 0, n - win), BAND)
        return c, q0, ks

    def scores(c, q0, ks, pair):
        valid = jnp.abs((q0 + qi) - (ks + ki)) <= BAND
        cs = slice(pair * LANE, (pair + 1) * LANE)
        qp = q_ref[0, c, pl.ds(q0, bq), cs]
        kp = k_ref[0, c, pl.ds(ks, win), cs]
        return [jnp.where(valid, _dot_nt(jnp.where(sel, qp, jnp.zeros_like(qp)), kp), NEG)
                for sel in (low_head, ~low_head)]

    def finish(c, q0, ks, pair, s_pair, lse_row):
        cs = slice(pair * LANE, (pair + 1) * LANE)
        vext = jnp.concatenate([v_ref[0, c, pl.ds(ks, win), cs], ones], axis=1)
        halves = []
        for half, s in enumerate(s_pair):
            m = jnp.max(s, axis=-1, keepdims=True)
            p = jnp.exp((s - m).astype(BF16))
            pv = _dot(p, vext)
            l = pv[:, LANE:]
            halves.append(pv[:, :LANE] / l)
            lse_row = jnp.where(lse_group == 2 * pair + half, m + jnp.log(l), lse_row)
        o_ref[0, c, pl.ds(q0, bq), cs] = jnp.where(low_head, halves[0], halves[1]).astype(BF16)
        return lse_row

    nblk = n // bq
    npair = ATT_SLOTS // 2
    depth = 4

    def step(i, carry):
        tasks = [(*where_block(i, u), pair) for u in range(unroll) for pair in range(npair)]
        ahead = [scores(*task) for task in tasks[:depth]]
        lse_row = None
        for t, task in enumerate(tasks):
            if t + depth < len(tasks):
                ahead.append(scores(*tasks[t + depth]))
            lse_row = finish(*task, ahead.pop(0), jnp.zeros((bq, LANE), F32) if task[3] == 0 else lse_row)
            if task[3] == npair - 1:
                lse_ref[0, task[0], pl.ds(task[1], bq), :] = lse_row
        return carry
    lax.fori_loop(0, dil * nblk // unroll, step, 0)


def _dil_attn(q, k, v, *, dil, bq=128):
    batch, _, n, _ = q.shape
    bq = min(bq, n)
    win = min(n, bq + 2 * BAND)
    unroll = max(1, ATTN_KEYS_PER_ITER // win)
    blk = lambda w: pl.BlockSpec((1, dil, n, w), lambda b: (b, 0, 0, 0))
    return pl.pallas_call(
        functools.partial(_attn_body, dil=dil, n=n, bq=bq, win=win, unroll=unroll),
        grid=(batch,),
        in_specs=[blk(ATT_OUT)] * 3,
        out_specs=[blk(ATT_OUT), blk(LANE)],
        out_shape=[jax.ShapeDtypeStruct((batch, dil, n, ATT_OUT), BF16),
                   jax.ShapeDtypeStruct((batch, dil, n, LANE), F32)],
        compiler_params=_params("parallel"),
        name=f"dil_attn_r{dil}",
    )(q, k, v)


def _mlstm_body(q_ref, kt_ref, v_ref, og_ref, gc_ref, gr_ref, gain_ref, o_ref,
                hf_ref, hb_ref, cn_ref, m_ref, bcol_ref, brow_ref, gtot_ref, *, seq, chunk, row_block=256):
    assert chunk == LANE
    nrb = seq // row_block
    cn_ref[...] = jnp.zeros_like(cn_ref)
    m_ref[...] = jnp.zeros_like(m_ref)

    nc = seq // chunk
    ri = lax.broadcasted_iota(jnp.int32, (chunk, chunk), 0)
    ci = lax.broadcasted_iota(jnp.int32, (chunk, chunk), 1)
    lower = ri >= ci
    upper = ri <= ci
    lower_f = lower.astype(F32)
    upper_f = upper.astype(F32)

    ones = jnp.ones((chunk, LANE), BF16)

    bwd_lane = lax.broadcasted_iota(jnp.int32, (chunk, LANE), 1) >= 2 * ML_HEADS
    bwd_sublane = lax.broadcasted_iota(jnp.int32, (ML_GATES, chunk), 0) >= 2 * ML_HEADS

    lf_cols = [_log_sigmoid(gc_ref[0, c * chunk:(c + 1) * chunk, :]) for c in range(nc)]
    pre_cols = _dot_exact(lower_f, jnp.concatenate(lf_cols, axis=1))
    for c in range(nc):
        pre = pre_cols[:, c * LANE:(c + 1) * LANE]
        bcol_ref[c * chunk:(c + 1) * chunk, :] = jnp.where(bwd_lane, pre[chunk - 1:chunk, :] - pre + lf_cols[c], pre)
    lf_rows = [_log_sigmoid(gr_ref[0, :, c * chunk:(c + 1) * chunk]) for c in range(nc)]
    pre_rows = _dot_exact(jnp.concatenate(lf_rows, axis=0), upper_f)
    for c in range(nc):
        pre = pre_rows[c * ML_GATES:(c + 1) * ML_GATES, :]
        tot = pre[:, chunk - 1:chunk]
        brow_ref[c] = jnp.where(bwd_sublane, tot - pre + lf_rows[c], pre)
        gtot_ref[c] = jnp.broadcast_to(tot, (ML_GATES, chunk))

    def step(it, carry):
        units = []
        for d in range(2):
            cidx = it if d == 0 else nc - 1 - it
            rows = pl.ds(pl.multiple_of(cidx * chunk, chunk), chunk)
            bcol = bcol_ref[rows, :]
            brow = brow_ref[cidx]
            grow = gr_ref[0, :, rows]
            gtot = gtot_ref[cidx]
            for hd in range(ML_HEADS):
                i_gate = (2 * d) * ML_HEADS + hd
                f_gate = (2 * d + 1) * ML_HEADS + hd
                units.append(dict(
                    st=d * ML_HEADS + hd, d=d, hd=hd, cidx=cidx, rows=rows,
                    hs=slice(hd * LANE, (hd + 1) * LANE),
                    b_col=bcol[:, f_gate:f_gate + 1],
                    w_row=brow[f_gate:f_gate + 1, :] - grow[i_gate:i_gate + 1, :],
                    gtot=gtot[f_gate:f_gate + 1, :]))

        for u in units:
            u["q"] = q_ref[0, u["rows"], u["hs"]]
            u["kt"] = kt_ref[0, u["hd"], u["cidx"]]
            u["s"] = _dot(u["q"], u["kt"])
            u["qcn"] = _dot(u["q"], cn_ref[u["st"]].astype(BF16))
        for u in units:
            u["m_row"] = m_ref[u["st"]][0:1, :]
            m_prev = jnp.max(u["m_row"], axis=-1, keepdims=True)
            g = jnp.where(lower if u["d"] == 0 else upper, -u["w_row"], NEG)
            rm = jnp.maximum(jnp.max(g, axis=-1, keepdims=True), m_prev)
            sc = u["s"] * jnp.exp(g - rm)
            u["vext"] = jnp.concatenate([v_ref[0, u["rows"], u["hs"]], ones], axis=1)
            u["nd"] = _dot(sc.astype(BF16), u["vext"])
            u["w_inter"] = jnp.exp(m_prev - rm)
            u["floor"] = jnp.exp(-(u["b_col"] + rm))
        for u in units:
            a_row = u["gtot"] - u["w_row"]
            m_new = jnp.maximum(u["gtot"] + u["m_row"], jnp.max(a_row, axis=-1, keepdims=True))
            decay = jnp.exp(u["gtot"] + u["m_row"] - m_new)
            ktw = (u["kt"].astype(F32) * jnp.exp(a_row - m_new)).astype(BF16)
            cn_ref[u["st"]] = (jnp.concatenate([decay, decay], axis=1) * cn_ref[u["st"]]
                               + _dot(ktw, u["vext"]))
            m_ref[u["st"]] = jnp.broadcast_to(m_new, m_ref.shape[1:])
        for u in units:
            tot = u["nd"] + u["w_inter"] * u["qcn"]
            hout = tot[:, :LANE] / jnp.maximum(jnp.abs(tot[:, LANE:]), u["floor"])
            if u["d"] == 0:
                hf_ref[u["rows"], u["hs"]] = hout
            else:
                hb_ref[u["rows"], u["hs"]] = hout
        return carry

    lax.fori_loop(0, nc, step, 0)

    def finish(i, carry):
        rows = pl.ds(pl.multiple_of(i * row_block, row_block), row_block)
        for hd in range(ML_HEADS):
            hs = slice(hd * LANE, (hd + 1) * LANE)
            hm = _rms(hf_ref[rows, hs] + hb_ref[rows, hs]) * gain_ref[:, hs]
            o_ref[0, rows, hs] = (hm * og_ref[0, rows, hs].astype(F32)).astype(BF16)
        return carry
    lax.fori_loop(0, nrb, finish, 0)


def _mlstm(mq, mkt, mv, og, gates_col, gates_row, out_gain, *, batch, seq, chunk=ML_CHUNK):
    b3 = lambda a: a.reshape(batch, seq, a.shape[-1])
    per_b = lambda w: pl.BlockSpec((1, seq, w), lambda b: (b, 0, 0))
    y = pl.pallas_call(
        functools.partial(_mlstm_body, seq=seq, chunk=chunk),
        grid=(batch,),
        in_specs=[per_b(ML_WIDTH), pl.BlockSpec((1,) + mkt.shape[1:], lambda b: (b, 0, 0, 0, 0)),
                  per_b(ML_WIDTH), per_b(ML_WIDTH), per_b(LANE),
                  pl.BlockSpec((1, ML_GATES, seq), lambda b: (b, 0, 0)), _resident(out_gain.shape)],
        out_specs=per_b(ML_WIDTH),
        out_shape=jax.ShapeDtypeStruct((batch, seq, ML_WIDTH), BF16),
        scratch_shapes=[
            pltpu.VMEM((seq, ML_WIDTH), F32),
            pltpu.VMEM((seq, ML_WIDTH), F32),
            pltpu.VMEM((2 * ML_HEADS, ML_HEAD_DIM, 2 * LANE), F32),
            pltpu.VMEM((2 * ML_HEADS, 8, LANE), F32),
            pltpu.VMEM((seq, LANE), F32),
            pltpu.VMEM((seq // chunk, ML_GATES, chunk), F32),
            pltpu.VMEM((seq // chunk, ML_GATES, chunk), F32),
        ],
        compiler_params=_params("parallel"),
        name="mlstm",
    )(b3(mq), mkt, b3(mv), b3(og), b3(gates_col), gates_row, out_gain)
    return y.reshape(batch * seq, ML_WIDTH)


def _mem_kv_body(mem_ref, gn_ref, w_ref, gk_ref, ck_ref, cv_ref):
    h = (_rms(mem_ref[0]) * gn_ref[...]).astype(BF16)
    kv = _dot(h, w_ref[...])
    for hd in range(CX_HEADS):
        cs = slice(hd * CX_HEAD_DIM, (hd + 1) * CX_HEAD_DIM)
        ck_ref[0, :, cs] = (_rms(kv[:, cs]) * gk_ref[...]).astype(BF16)
    cv_ref[0] = kv[:, CX_WIDTH:].astype(BF16)


def _mem_kv(mem, gn, w, gk):
    b, mlen, dm = mem.shape
    blk = lambda w_: pl.BlockSpec((1, mlen, w_), lambda i: (i, 0, 0))
    return pl.pallas_call(
        _mem_kv_body,
        grid=(b,),
        in_specs=[blk(dm), _resident((1, dm)), _resident(w.shape), _resident(gk.shape)],
        out_specs=[blk(CX_WIDTH)] * 2,
        out_shape=[jax.ShapeDtypeStruct((b, mlen, CX_WIDTH), BF16)] * 2,
        compiler_params=_params("parallel"),
        name="mem_kv",
    )(mem, gn, w, gk)


def _merge_body(x_ref, o0_ref, o1_ref, o2_ref, l0_ref, l1_ref, l2_ref, yml_ref, cq_ref, g_ref, ck_ref, cv_ref,
                wa_ref, wm_ref, wc_ref, wout_ref, ex_ref, out_ref, nat_ref):
    tm, dm = x_ref.shape
    nslab = ATT_OUT // LANE

    def natural(ref, dil, width):
        if dil == 1:
            return ref[0, 0].astype(F32)
        m = tm // dil
        for c in range(dil):
            for sl in range(width // LANE):
                nat_ref[sl, pl.ds(c, m, stride=dil), :] = ref[0, c, :, sl * LANE:(sl + 1) * LANE].astype(F32)
        return jnp.concatenate([nat_ref[sl] for sl in range(width // LANE)], axis=1)

    dils = [dil for _, dil in DIL_GROUPS]
    lses = [natural(ref, dil, LANE) for ref, dil in zip((l0_ref, l1_ref, l2_ref), dils)]
    top = jnp.maximum(jnp.maximum(lses[0], lses[1]), lses[2])
    es = [jnp.exp(l - top) for l in lses]
    inv = 1.0 / (es[0] + es[1] + es[2])
    ex = ex_ref[...]
    y_att = jnp.zeros((tm, ATT_OUT), F32)
    for e, o_ref, dil in zip(es, (o0_ref, o1_ref, o2_ref), dils):
        alpha = e * inv
        hi = alpha.astype(BF16)
        lo = (alpha - hi.astype(F32)).astype(BF16)
        y_att = y_att + (_dot(hi, ex) + _dot(lo, ex)) * natural(o_ref, dil, nslab * LANE)

    y_cx = []
    for hd in range(CX_HEADS):
        cs = slice(hd * CX_HEAD_DIM, (hd + 1) * CX_HEAD_DIM)
        s = _dot_nt(cq_ref[:, cs], ck_ref[0, :, cs])
        p = jnp.exp(s - jnp.max(s, axis=-1, keepdims=True))
        l = jnp.sum(p, axis=-1, keepdims=True)
        y_cx.append(_dot(p.astype(BF16), cv_ref[0, :, cs]) / l)
    y_cx = jnp.concatenate(y_cx, axis=1)

    merged = (g_ref[:, 0:dm].astype(F32) * _dot(y_att.astype(BF16), wa_ref[...])
              + g_ref[:, dm:2 * dm].astype(F32) * _dot(yml_ref[...], wm_ref[...])
              + g_ref[:, 2 * dm:3 * dm].astype(F32) * _dot(y_cx.astype(BF16), wc_ref[...]))
    out_ref[...] = x_ref[...] + _dot(merged.astype(BF16), wout_ref[...])


def _merge(x, o_att, lse_att, y_ml, cq, gates, ck, cv, wa, wm, wc, wout, expand, *, seq, tm=512):
    t, dm = x.shape
    mlen = ck.shape[1]
    nseq = seq // tm
    tok = lambda w: pl.BlockSpec((tm, w), lambda i: (i, 0))
    mem = pl.BlockSpec((1, mlen, CX_WIDTH), lambda i: (i // nseq, 0, 0))
    res = lambda dil, w: pl.BlockSpec((1, dil, tm // dil, w), lambda i: (i // nseq, 0, i % nseq, 0))
    dils = [dil for _, dil in DIL_GROUPS]
    return pl.pallas_call(
        _merge_body,
        grid=(t // tm,),
        in_specs=[tok(dm)] + [res(dil, ATT_OUT) for dil in dils] + [res(dil, LANE) for dil in dils]
                 + [tok(ML_WIDTH), tok(CX_WIDTH), tok(N_BRANCH * dm), mem, mem,
                    _resident(wa.shape), _resident(wm.shape), _resident(wc.shape), _resident(wout.shape),
                    _resident(expand.shape)],
        out_specs=tok(dm),
        out_shape=jax.ShapeDtypeStruct((t, dm), F32),
        scratch_shapes=[pltpu.VMEM((ATT_OUT // LANE, tm, LANE), F32)],
        compiler_params=_params("parallel"),
        name="merge",
    )(x, *o_att, *lse_att, y_ml, cq, gates, ck, cv, wa, wm, wc, wout, expand)


def _rope_tables(seq, tm):
    d = ATT_HEAD_DIM
    inv = ROPE_THETA ** (-jnp.arange(0, d, 2, dtype=F32) / d)
    ang = jnp.arange(seq, dtype=F32)[:, None] * inv[None, :]
    cos, sin = jnp.cos(ang), jnp.sin(ang)
    cos_t = jnp.tile(jnp.concatenate([cos, cos], axis=-1), (1, LANE // d))
    sin_t = jnp.tile(jnp.concatenate([-sin, sin], axis=-1), (1, LANE // d))
    tables = []
    for _, dil in DIL_GROUPS:
        perm = lambda a: a.reshape(seq // tm, tm // dil, dil, -1).swapaxes(1, 2).reshape(seq, -1)
        tables += [perm(cos_t), perm(sin_t)]
    return tables


def _layer(x, mem, p):
    batch, seq, dm = x.shape
    xt = x.reshape(batch * seq, dm)
    row = lambda a: a.reshape(1, -1).astype(F32)
    bf = lambda a: a.astype(BF16)

    x1 = _ffn(xt, row(p["norm_ffn1"]), bf(p["w_ffn1_in"]), bf(p["w_ffn1_out"]), row(p["norm_final"]),
              final_norm=False)

    w_in = bf(p["w_in"])
    sizes = (ATT_QKV, ATT_QKV, ATT_QKV, ML_WIDTH, ML_WIDTH, ML_WIDTH, ML_WIDTH, ML_GATES, CX_WIDTH, N_BRANCH * dm)
    offs = [0]
    for s in sizes:
        offs.append(offs[-1] + s)
    col = lambda i, j=None: w_in[:, offs[i]:offs[(i if j is None else j) + 1]]
    gn = row(p["norm_mix"])

    idx = jnp.arange(MXU_DIM) // ATT_HEAD_DIM
    e64 = jnp.where(idx[:, None] == idx[None, :], 1.0 / ATT_HEAD_DIM, 0.0).astype(BF16)
    tile8 = lambda a: jnp.tile(row(a), (1, ATT_SLOTS))
    qkv = _qkv_proj(x1, gn, col(0), col(1), col(2), tile8(p["att_q_gain"]), tile8(p["att_k_gain"]),
                    _rope_tables(seq, QKV_TILE), e64, batch=batch, seq=seq, tm=QKV_TILE)

    w_if = jnp.pad(col(7), ((0, 0), (0, LANE - ML_GATES)))
    b_if = jnp.pad(row(p["ml_gate_b"]), ((0, 0), (0, LANE - ML_GATES)))
    mq, mkt, mv, og, mif, gates_row, cq, gates = _misc_proj(
        x1, gn, col(3, 4), col(5), col(6), w_if, b_if, col(8), row(p["cx_q_gain"]), col(9),
        row(p["mix_gate_b"]), p["ml_conv_w"].astype(F32), row(p["ml_conv_b"]), batch=batch, seq=seq)

    o_att, lse_att = [], []
    for g, (_, dil) in enumerate(DIL_GROUPS):
        o, lse = _dil_attn(qkv[g], qkv[3 + g], qkv[6 + g], dil=dil)
        o_att.append(o)
        lse_att.append(lse)

    y_ml = _mlstm(mq, mkt, mv, og, mif, gates_row, row(p["ml_out_gain"]), batch=batch, seq=seq)

    ck, cv = _mem_kv(mem, row(p["norm_mem"]), bf(p["w_mem_kv"]), row(p["cx_k_gain"]))

    lanes = jnp.arange(ATT_OUT) // ATT_HEAD_DIM * LSE_LANES_PER_HEAD
    expand = (jnp.arange(LANE)[:, None] == lanes[None, :]).astype(BF16)
    x2 = _merge(x1, o_att, lse_att, y_ml, cq, gates, ck, cv, bf(p["w_br_att"]), bf(p["w_br_ml"]),
                bf(p["w_br_cx"]), bf(p["w_out"]), expand, seq=seq)

    x3 = _ffn(x2, row(p["norm_ffn2"]), bf(p["w_ffn2_in"]), bf(p["w_ffn2_out"]), row(p["norm_final"]),
              final_norm=True)
    return x3.reshape(batch, seq, dm)


def kernel(x, mem, norm_ffn1, w_ffn1_in, w_ffn1_out, norm_mix, norm_mem, w_in, att_q_gain, att_k_gain,
           ml_conv_w, ml_conv_b, ml_gate_b, ml_out_gain, cx_q_gain, cx_k_gain, w_mem_kv, mix_gate_b,
           w_br_att, w_br_ml, w_br_cx, w_out, norm_ffn2, w_ffn2_in, w_ffn2_out, norm_final):
    params = dict(norm_ffn1=norm_ffn1, w_ffn1_in=w_ffn1_in, w_ffn1_out=w_ffn1_out, norm_mix=norm_mix,
                  norm_mem=norm_mem, w_in=w_in, att_q_gain=att_q_gain, att_k_gain=att_k_gain,
                  ml_conv_w=ml_conv_w, ml_conv_b=ml_conv_b, ml_gate_b=ml_gate_b, ml_out_gain=ml_out_gain,
                  cx_q_gain=cx_q_gain, cx_k_gain=cx_k_gain, w_mem_kv=w_mem_kv, mix_gate_b=mix_gate_b,
                  w_br_att=w_br_att, w_br_ml=w_br_ml, w_br_cx=w_br_cx, w_out=w_out, norm_ffn2=norm_ffn2,
                  w_ffn2_in=w_ffn2_in, w_ffn2_out=w_ffn2_out, norm_final=norm_final)
    depth = norm_ffn1.shape[0]
    for layer in range(depth):
        x = _layer(x, mem, {k: v[layer] for k, v in params.items()})
    return x
```

```python
import functools

import jax
import jax.numpy as jnp
from jax import lax
from jax.experimental import pallas as pl
from jax.experimental.pallas import tpu as pltpu

F32 = jnp.float32
BF16 = jnp.bfloat16

EPS = 1e-6
ROPE_THETA = 10000.0
NEG = -1e30

ATT_HEAD_DIM = 64
ATT_SLOTS = 8
DIL_GROUPS = ((128, 1), (512, 4), (2048, 16))
ATT_OUT = ATT_SLOTS * ATT_HEAD_DIM
ATT_QKV = len(DIL_GROUPS) * ATT_OUT
ML_HEADS = 4
ML_HEAD_DIM = 128
ML_WIDTH = ML_HEADS * ML_HEAD_DIM
ML_GATES = 4 * ML_HEADS
ML_CHUNK = 128
ML_CONV = 5
CX_HEADS = 4
CX_HEAD_DIM = 128
CX_WIDTH = CX_HEADS * CX_HEAD_DIM
N_BRANCH = 3

LANE = 128
MXU_DIM = 256
VMEM_LIMIT = 56 * 1024 * 1024

STAT_LANES_PER_HEAD = LANE // ATT_SLOTS
QKV_TILE = 512
ATTN_KEYS_PER_ITER = 512
BAND = 64
assert all((w // 2) // r == BAND for w, r in DIL_GROUPS)

NT_DIMS = (((1,), (1,)), ((), ()))
TN_DIMS = (((0,), (0,)), ((), ()))


def _dot(a, b):
    return jnp.dot(a, b, preferred_element_type=F32)


def _dot_nt(a, b):
    return lax.dot_general(a, b, NT_DIMS, preferred_element_type=F32)


def _dot_exact(a, b):
    return jnp.dot(a, b, preferred_element_type=F32, precision=lax.Precision.HIGHEST)


def _rms(x):
    return x * lax.rsqrt(jnp.mean(x * x, axis=-1, keepdims=True) + EPS)


def _sigmoid(x):
    return 1.0 / (1.0 + jnp.exp(-x))


def _log_sigmoid(x):
    return jnp.minimum(x, 0.0) - jnp.log(1.0 + jnp.exp(-jnp.abs(x)))


def _params(*sem):
    return pltpu.CompilerParams(dimension_semantics=sem, vmem_limit_bytes=VMEM_LIMIT)


def _resident(shape):
    return pl.BlockSpec(shape, lambda *_: (0,) * len(shape), pipeline_mode=pl.Buffered(1))


def _ffn_body(x_ref, g_ref, wi_ref, wo_ref, gf_ref, o_ref, h_ref, a_ref, *, final_norm, tf):
    h_ref[...] = (_rms(x_ref[...]) * g_ref[...]).astype(BF16)
    d_ff = wo_ref.shape[0]
    for j in range(0, d_ff, tf):
        h = h_ref[...]
        gate = _dot(h, wi_ref[:, j:j + tf])
        up = _dot(h, wi_ref[:, d_ff + j:d_ff + j + tf])
        a_ref[:, j:j + tf] = (gate * _sigmoid(gate) * up).astype(BF16)
    y = x_ref[...] + 0.5 * _dot(a_ref[...], wo_ref[...])
    if final_norm:
        y = _rms(y) * gf_ref[...]
    o_ref[...] = y


def _ffn(x, gain, w_in, w_out, final_gain, *, final_norm, tm, tf=MXU_DIM):
    t, dm = x.shape
    d_ff = w_out.shape[0]
    return pl.pallas_call(
        functools.partial(_ffn_body, final_norm=final_norm, tf=tf),
        grid=(t // tm,),
        in_specs=[pl.BlockSpec((tm, dm), lambda i: (i, 0)), _resident((1, dm)), _resident(w_in.shape),
                  _resident(w_out.shape), _resident((1, dm))],
        out_specs=pl.BlockSpec((tm, dm), lambda i: (i, 0)),
        out_shape=jax.ShapeDtypeStruct((t, dm), F32),
        scratch_shapes=[pltpu.VMEM((tm, dm), BF16), pltpu.VMEM((tm, d_ff), BF16)],
        compiler_params=_params("parallel"),
        name="ffn_final" if final_norm else "ffn",
    )(x, gain, w_in, w_out, final_gain)


def _qkv_body(x_ref, gn_ref, wq_ref, wk_ref, wv_ref, gq_ref, gk_ref, cos0_ref, sin0_ref, cos1_ref, sin1_ref,
              cos2_ref, sin2_ref, e_ref, *rest):
    outs, (hs_ref, hp_ref) = rest[:9], rest[9:]
    q_outs, k_outs, v_outs = outs[0:3], outs[3:6], outs[6:9]
    tables = ((cos0_ref, sin0_ref), (cos1_ref, sin1_ref), (cos2_ref, sin2_ref))
    tm, dm = x_ref.shape
    hn = _rms(x_ref[...]) * gn_ref[...]
    for sl in range(dm // LANE):
        hs_ref[sl] = hn[:, sl * LANE:(sl + 1) * LANE]
    lane = lax.broadcasted_iota(jnp.int32, (tm, ATT_OUT), 1)
    first_half = (lane & (ATT_HEAD_DIM - 1)) < ATT_HEAD_DIM // 2
    e = e_ref[...]
    for g, (_, dil) in enumerate(DIL_GROUPS):
        m = tm // dil
        if dil == 1:
            h = hn.astype(BF16)
        else:
            for c in range(dil):
                for sl in range(dm // LANE):
                    hp_ref[c * m:(c + 1) * m, sl * LANE:(sl + 1) * LANE] = (
                        hs_ref[sl, pl.ds(c, m, stride=dil), :].astype(BF16))
            h = hp_ref[...]
        cos = jnp.concatenate([tables[g][0][...]] * (ATT_OUT // LANE), axis=1)
        sin = jnp.concatenate([tables[g][1][...]] * (ATT_OUT // LANE), axis=1)
        cs = slice(g * ATT_OUT, (g + 1) * ATT_OUT)

        def emit(out, val):
            for c in range(dil):
                out[0, c] = val[c * m:(c + 1) * m, :].astype(BF16)

        for w_ref, gain_ref, scale, out in ((wq_ref, gq_ref, ATT_HEAD_DIM ** -0.5, q_outs[g]),
                                            (wk_ref, gk_ref, 1.0, k_outs[g])):
            t = _dot(h, w_ref[:, cs])
            sq = (t * t).astype(BF16)
            ms = jnp.concatenate([_dot(sq[:, c:c + MXU_DIM], e) for c in range(0, ATT_OUT, MXU_DIM)], axis=1)
            tn = t * lax.rsqrt(ms + EPS) * gain_ref[...]
            partner = jnp.where(first_half,
                                pltpu.roll(tn, ATT_OUT - ATT_HEAD_DIM // 2, 1),
                                pltpu.roll(tn, ATT_HEAD_DIM // 2, 1))
            emit(out, (tn * cos + partner * sin) * scale)
        emit(v_outs[g], _dot(h, wv_ref[:, cs]))


def _qkv_proj(x, gn, wq, wk, wv, gq, gk, tables, e64, *, batch, seq, tm=256):
    t, dm = x.shape
    nseq = seq // tm
    pos = pl.BlockSpec((tm, LANE), lambda i: (i % nseq, 0))
    out_specs, out_shape = [], []
    for _ in range(3):
        for _, dil in DIL_GROUPS:
            out_specs.append(pl.BlockSpec((1, dil, tm // dil, ATT_OUT), lambda i: (i // nseq, 0, i % nseq, 0)))
            out_shape.append(jax.ShapeDtypeStruct((batch, dil, seq // dil, ATT_OUT), BF16))
    return pl.pallas_call(
        _qkv_body,
        grid=(t // tm,),
        in_specs=[pl.BlockSpec((tm, dm), lambda i: (i, 0)), _resident((1, dm)), _resident((dm, ATT_QKV)),
                  _resident((dm, ATT_QKV)), _resident((dm, ATT_QKV)), _resident((1, ATT_OUT)),
                  _resident((1, ATT_OUT))] + [pos] * 6 + [_resident((MXU_DIM, MXU_DIM))],
        out_specs=out_specs,
        out_shape=out_shape,
        scratch_shapes=[pltpu.VMEM((dm // LANE, tm, LANE), F32),
                        pltpu.VMEM((tm, dm), BF16)],
        compiler_params=_params("parallel"),
        name="qkv_proj",
    )(x, gn, wq, wk, wv, gq, gk, *tables, e64)


def _misc_body(x_ref, xprev_ref, xnext_ref, gn_ref, wqk_ref, wv_ref, wo_ref, wif_ref, bif_ref, wcq_ref, gcq_ref,
               wg_ref, bg_ref, cw_ref, cb_ref,
               mq_ref, mkt_ref, mv_ref, og_ref, mif_ref, mift_ref, cq_ref, g_ref, hext_ref, pad_ref,
               *, nseq, chunk, row_block=256):
    tm, dm = x_ref.shape
    halo = xprev_ref.shape[0]
    norm = lambda ref: (_rms(ref[...]) * gn_ref[...]).astype(BF16)
    h = norm(x_ref)

    hext_ref[0:halo, :] = norm(xprev_ref)
    hext_ref[halo:halo + tm, :] = h
    hext_ref[halo + tm:, :] = norm(xnext_ref)
    pad_ref[...] = _dot(hext_ref[...], wqk_ref[...])
    pos = pl.program_id(0) % nseq
    pad_ref[0:halo, :] = pad_ref[0:halo, :] * (pos > 0).astype(F32)
    pad_ref[halo + tm:, :] = pad_ref[halo + tm:, :] * (pos < nseq - 1).astype(F32)

    def conv(c):
        cs = slice(c * LANE, (c + 1) * LANE)
        for r0 in range(0, tm, row_block):
            acc = jnp.zeros((row_block, LANE), F32) + cb_ref[:, cs]
            for j in range(ML_CONV):
                acc = acc + cw_ref[j:j + 1, cs] * pad_ref[pl.ds(halo + r0 - ML_CONV // 2 + j, row_block), cs]
            y = acc * _sigmoid(acc)
            if c < ML_HEADS:
                mq_ref[r0:r0 + row_block, cs] = (y * ML_HEAD_DIM ** -0.5).astype(BF16)
            else:
                yt = y.T
                for sub in range(0, row_block, chunk):
                    mkt_ref[0, c - ML_HEADS, (r0 + sub) // chunk] = yt[:, sub:sub + chunk].astype(BF16)

    def small_streams():
        mv_ref[...] = _dot(h, wv_ref[...]).astype(BF16)
        og_ref[...] = _sigmoid(_dot(h, wo_ref[...])).astype(BF16)
        mif = _dot(h, wif_ref[...]) + bif_ref[...]
        mif_ref[...] = mif
        mift_ref[0] = mif.T[:ML_GATES, :]

    def cross_q():
        t = _dot(h, wcq_ref[...])
        for hd in range(CX_HEADS):
            cs = slice(hd * CX_HEAD_DIM, (hd + 1) * CX_HEAD_DIM)
            cq_ref[:, cs] = (_rms(t[:, cs]) * gcq_ref[...] * CX_HEAD_DIM ** -0.5).astype(BF16)

    def mix_gate(cs):
        g_ref[:, cs] = _sigmoid(_dot(h, wg_ref[:, cs]) + bg_ref[:, cs]).astype(BF16)

    half = dm // 2
    matmuls = [small_streams, cross_q] + [functools.partial(mix_gate, slice(o, o + half))
                                          for o in range(0, N_BRANCH * dm, half)]
    assert len(matmuls) == 2 * ML_HEADS
    for c, matmul in enumerate(matmuls):
        matmul()
        conv(c)


def _misc_proj(x, gn, wqk, wv, wo, wif, bif, wcq, gcq, wg, bg, conv_w, conv_b, *, batch, seq, tm=512,
               chunk=ML_CHUNK):
    t, dm = x.shape
    nseq = seq // tm
    halo = 16
    per_halo = tm // halo
    tok = lambda w: pl.BlockSpec((tm, w), lambda i: (i, 0))
    prev = pl.BlockSpec((halo, dm), lambda i: (jnp.maximum(i * per_halo - 1, 0), 0))
    nxt = pl.BlockSpec((halo, dm), lambda i: (jnp.minimum((i + 1) * per_halo, t // halo - 1), 0))
    outs = [(tok(ML_WIDTH), (t, ML_WIDTH), BF16),
            (pl.BlockSpec((1, ML_HEADS, tm // chunk, ML_HEAD_DIM, chunk), lambda i: (i // nseq, 0, i % nseq, 0, 0)),
             (batch, ML_HEADS, seq // chunk, ML_HEAD_DIM, chunk), BF16),
            (tok(ML_WIDTH), (t, ML_WIDTH), BF16),
            (tok(ML_WIDTH), (t, ML_WIDTH), BF16), (tok(LANE), (t, LANE), F32),
            (pl.BlockSpec((1, ML_GATES, tm), lambda i: (i // nseq, 0, i % nseq)), (batch, ML_GATES, seq), F32),
            (tok(CX_WIDTH), (t, CX_WIDTH), BF16), (tok(N_BRANCH * dm), (t, N_BRANCH * dm), BF16)]
    return pl.pallas_call(
        functools.partial(_misc_body, nseq=nseq, chunk=chunk),
        grid=(t // tm,),
        in_specs=[tok(dm), prev, nxt, _resident((1, dm)), _resident(wqk.shape), _resident(wv.shape),
                  _resident(wo.shape), _resident(wif.shape), _resident(bif.shape), _resident(wcq.shape),
                  _resident(gcq.shape), _resident(wg.shape), _resident(bg.shape), _resident(conv_w.shape),
                  _resident(conv_b.shape)],
        out_specs=[spec for spec, _, _ in outs],
        out_shape=[jax.ShapeDtypeStruct(shape, dtype) for _, shape, dtype in outs],
        scratch_shapes=[pltpu.VMEM((tm + 2 * halo, dm), BF16),
                        pltpu.VMEM((tm + 2 * halo, 2 * ML_WIDTH), F32)],
        compiler_params=_params("parallel"),
        name="misc_proj",
    )(x, x, x, gn, wqk, wv, wo, wif, bif, wcq, gcq, wg, bg, conv_w, conv_b)


def _attn_body(q_ref, k_ref, v_ref, o_ref, m_ref, l_ref, *, dil, n, bq, win, unroll):
    lane = lax.broadcasted_iota(jnp.int32, (bq, LANE), 1)
    low_head = lane < ATT_HEAD_DIM
    stat_group = lane // STAT_LANES_PER_HEAD
    delta = (lax.broadcasted_iota(jnp.int32, (bq, win), 0)
             - lax.broadcasted_iota(jnp.int32, (bq, win), 1))
    ones = jnp.ones((win, LANE), BF16)

    def where_block(i, u):
        blk = i * unroll + u
        c = blk // nblk
        q0 = pl.multiple_of((blk % nblk) * bq, bq)
        ks = pl.multiple_of(jnp.clip(q0 - BAND, 0, n - win), BAND)
        valid = lax.bitcast_convert_type(delta + (q0 - ks + BAND), jnp.uint32) <= jnp.uint32(2 * BAND)
        return c, q0, ks, valid

    def scores(c, q0, ks, valid, pair):
        cs = slice(pair * LANE, (pair + 1) * LANE)
        qp = q_ref[0, c, pl.ds(q0, bq), cs]
        kp = k_ref[0, c, pl.ds(ks, win), cs]
        return [jnp.where(valid, _dot_nt(jnp.where(sel, qp, jnp.zeros_like(qp)), kp), NEG)
                for sel in (low_head, ~low_head)]

    def finish(c, q0, ks, _, pair, s_pair, stats):
        cs = slice(pair * LANE, (pair + 1) * LANE)
        vext = jnp.concatenate([v_ref[0, c, pl.ds(ks, win), cs], ones], axis=1)
        m_row, l_row = stats
        halves = []
        for half, s in enumerate(s_pair):
            m = jnp.max(s, axis=-1, keepdims=True)
            p = jnp.exp((s - m).astype(BF16))
            pv = _dot(p, vext)
            halves.append(pv[:, :LANE])
            here = stat_group == 2 * pair + half
            m_row = jnp.where(here, m, m_row)
            l_row = jnp.where(here, pv[:, LANE:], l_row)
        o_ref[0, c, pl.ds(q0, bq), cs] = jnp.where(low_head, halves[0], halves[1]).astype(BF16)
        return m_row, l_row

    nblk = n // bq
    npair = ATT_SLOTS // 2
    depth = 4

    def step(i, carry):
        blocks = [where_block(i, u) for u in range(unroll)]
        tasks = [(*block, pair) for block in blocks for pair in range(npair)]
        ahead = [scores(*task) for task in tasks[:depth]]
        zeros = jnp.zeros((bq, LANE), F32)
        stats = None
        for t, task in enumerate(tasks):
            if t + depth < len(tasks):
                ahead.append(scores(*tasks[t + depth]))
            stats = finish(*task, ahead.pop(0), (zeros, zeros) if task[-1] == 0 else stats)
            if task[-1] == npair - 1:
                m_ref[0, task[0], pl.ds(task[1], bq), :] = stats[0]
                l_ref[0, task[0], pl.ds(task[1], bq), :] = stats[1]
        return carry
    lax.fori_loop(0, dil * nblk // unroll, step, 0)


def _dil_attn(q, k, v, *, dil, bq=128):
    batch, _, n, _ = q.shape
    bq = min(bq, n)
    win = min(n, bq + 2 * BAND)
    unroll = max(1, ATTN_KEYS_PER_ITER // win)
    blk = lambda w: pl.BlockSpec((1, dil, n, w), lambda b: (b, 0, 0, 0))
    return pl.pallas_call(
        functools.partial(_attn_body, dil=dil, n=n, bq=bq, win=win, unroll=unroll),
        grid=(batch,),
        in_specs=[blk(ATT_OUT)] * 3,
        out_specs=[blk(ATT_OUT), blk(LANE), blk(LANE)],
        out_shape=[jax.ShapeDtypeStruct((batch, dil, n, ATT_OUT), BF16),
                   jax.ShapeDtypeStruct((batch, dil, n, LANE), F32),
                   jax.ShapeDtypeStruct((batch, dil, n, LANE), F32)],
        compiler_params=_params("parallel"),
        name=f"dil_attn_r{dil}",
    )(q, k, v)


def _mlstm_body(q_ref, kt_ref, v_ref, og_ref, gc_ref, gr_ref, gain_ref, o_ref,
                hf_ref, hb_ref, cn_ref, m_ref, bcol_ref, brow_ref, gtot_ref, *, seq, chunk, row_block=256):
    assert chunk == LANE
    nrb = seq // row_block
    cn_ref[...] = jnp.zeros_like(cn_ref)
    m_ref[...] = jnp.zeros_like(m_ref)

    nc = seq // chunk
    ri = lax.broadcasted_iota(jnp.int32, (chunk, chunk), 0)
    ci = lax.broadcasted_iota(jnp.int32, (chunk, chunk), 1)
    lower = ri >= ci
    upper = ri <= ci
    lower_f = lower.astype(F32)
    upper_f = upper.astype(F32)

    ones = jnp.ones((chunk, LANE), BF16)

    bwd_lane = lax.broadcasted_iota(jnp.int32, (chunk, LANE), 1) >= 2 * ML_HEADS
    bwd_sublane = lax.broadcasted_iota(jnp.int32, (ML_GATES, chunk), 0) >= 2 * ML_HEADS

    lf_cols = [_log_sigmoid(gc_ref[0, c * chunk:(c + 1) * chunk, :]) for c in range(nc)]
    pre_cols = _dot_exact(lower_f, jnp.concatenate(lf_cols, axis=1))
    for c in range(nc):
        pre = pre_cols[:, c * LANE:(c + 1) * LANE]
        bcol_ref[c * chunk:(c + 1) * chunk, :] = jnp.where(bwd_lane, pre[chunk - 1:chunk, :] - pre + lf_cols[c], pre)
    lf_rows = [_log_sigmoid(gr_ref[0, :, c * chunk:(c + 1) * chunk]) for c in range(nc)]
    pre_rows = _dot_exact(jnp.concatenate(lf_rows, axis=0), upper_f)
    for c in range(nc):
        pre = pre_rows[c * ML_GATES:(c + 1) * ML_GATES, :]
        tot = pre[:, chunk - 1:chunk]
        brow_ref[c] = jnp.where(bwd_sublane, tot - pre + lf_rows[c], pre)
        gtot_ref[c] = jnp.broadcast_to(tot, (ML_GATES, chunk))

    def step(it, carry):
        units = []
        for d in range(2):
            cidx = it if d == 0 else nc - 1 - it
            rows = pl.ds(pl.multiple_of(cidx * chunk, chunk), chunk)
            bcol = bcol_ref[rows, :]
            brow = brow_ref[cidx]
            grow = gr_ref[0, :, rows]
            gtot = gtot_ref[cidx]
            for hd in range(ML_HEADS):
                i_gate = (2 * d) * ML_HEADS + hd
                f_gate = (2 * d + 1) * ML_HEADS + hd
                units.append(dict(
                    st=d * ML_HEADS + hd, d=d, hd=hd, cidx=cidx, rows=rows,
                    hs=slice(hd * LANE, (hd + 1) * LANE),
                    b_col=bcol[:, f_gate:f_gate + 1],
                    w_row=brow[f_gate:f_gate + 1, :] - grow[i_gate:i_gate + 1, :],
                    gtot=gtot[f_gate:f_gate + 1, :]))

        for u in units:
            u["q"] = q_ref[0, u["rows"], u["hs"]]
            u["kt"] = kt_ref[0, u["hd"], u["cidx"]]
            u["s"] = _dot(u["q"], u["kt"])
            u["qcn"] = _dot(u["q"], cn_ref[u["st"]].astype(BF16))
        for u in units:
            u["m_row"] = m_ref[u["st"]][0:1, :]
            m_prev = jnp.max(u["m_row"], axis=-1, keepdims=True)
            g = jnp.where(lower if u["d"] == 0 else upper, -u["w_row"], NEG)
            rm = jnp.maximum(jnp.max(g, axis=-1, keepdims=True), m_prev)
            sc = u["s"] * jnp.exp(g - rm)
            u["vext"] = jnp.concatenate([v_ref[0, u["rows"], u["hs"]], ones], axis=1)
            u["nd"] = _dot(sc.astype(BF16), u["vext"])
            u["w_inter"] = jnp.exp(m_prev - rm)
            u["floor"] = jnp.exp(-(u["b_col"] + rm))
        for u in units:
            a_row = u["gtot"] - u["w_row"]
            m_new = jnp.maximum(u["gtot"] + u["m_row"], jnp.max(a_row, axis=-1, keepdims=True))
            decay = jnp.exp(u["gtot"] + u["m_row"] - m_new)
            ktw = (u["kt"].astype(F32) * jnp.exp(a_row - m_new)).astype(BF16)
            cn_ref[u["st"]] = (jnp.concatenate([decay, decay], axis=1) * cn_ref[u["st"]]
                               + _dot(ktw, u["vext"]))
            m_ref[u["st"]] = jnp.broadcast_to(m_new, m_ref.shape[1:])
        for u in units:
            tot = u["nd"] + u["w_inter"] * u["qcn"]
            hout = tot[:, :LANE] / jnp.maximum(jnp.abs(tot[:, LANE:]), u["floor"])
            if u["d"] == 0:
                hf_ref[u["rows"], u["hs"]] = hout
            else:
                hb_ref[u["rows"], u["hs"]] = hout
        return carry

    lax.fori_loop(0, nc, step, 0)

    def finish(i, carry):
        rows = pl.ds(pl.multiple_of(i * row_block, row_block), row_block)
        for hd in range(ML_HEADS):
            hs = slice(hd * LANE, (hd + 1) * LANE)
            hm = _rms(hf_ref[rows, hs] + hb_ref[rows, hs]) * gain_ref[:, hs]
            o_ref[0, rows, hs] = (hm * og_ref[0, rows, hs].astype(F32)).astype(BF16)
        return carry
    lax.fori_loop(0, nrb, finish, 0)


def _mlstm(mq, mkt, mv, og, gates_col, gates_row, out_gain, *, batch, seq, chunk=ML_CHUNK):
    b3 = lambda a: a.reshape(batch, seq, a.shape[-1])
    per_b = lambda w: pl.BlockSpec((1, seq, w), lambda b: (b, 0, 0))
    y = pl.pallas_call(
        functools.partial(_mlstm_body, seq=seq, chunk=chunk),
        grid=(batch,),
        in_specs=[per_b(ML_WIDTH), pl.BlockSpec((1,) + mkt.shape[1:], lambda b: (b, 0, 0, 0, 0)),
                  per_b(ML_WIDTH), per_b(ML_WIDTH), per_b(LANE),
                  pl.BlockSpec((1, ML_GATES, seq), lambda b: (b, 0, 0)), _resident(out_gain.shape)],
        out_specs=per_b(ML_WIDTH),
        out_shape=jax.ShapeDtypeStruct((batch, seq, ML_WIDTH), BF16),
        scratch_shapes=[
            pltpu.VMEM((seq, ML_WIDTH), F32),
            pltpu.VMEM((seq, ML_WIDTH), F32),
            pltpu.VMEM((2 * ML_HEADS, ML_HEAD_DIM, 2 * LANE), F32),
            pltpu.VMEM((2 * ML_HEADS, 8, LANE), F32),
            pltpu.VMEM((seq, LANE), F32),
            pltpu.VMEM((seq // chunk, ML_GATES, chunk), F32),
            pltpu.VMEM((seq // chunk, ML_GATES, chunk), F32),
        ],
        compiler_params=_params("parallel"),
        name="mlstm",
    )(b3(mq), mkt, b3(mv), b3(og), b3(gates_col), gates_row, out_gain)
    return y.reshape(batch * seq, ML_WIDTH)


def _mem_kv_body(mem_ref, gn_ref, w_ref, gk_ref, ck_ref, cv_ref):
    h = (_rms(mem_ref[0]) * gn_ref[...]).astype(BF16)
    kv = _dot(h, w_ref[...])
    for hd in range(CX_HEADS):
        cs = slice(hd * CX_HEAD_DIM, (hd + 1) * CX_HEAD_DIM)
        ck_ref[0, :, cs] = (_rms(kv[:, cs]) * gk_ref[...]).astype(BF16)
    cv_ref[0] = kv[:, CX_WIDTH:].astype(BF16)


def _mem_kv(mem, gn, w, gk):
    b, mlen, dm = mem.shape
    blk = lambda w_: pl.BlockSpec((1, mlen, w_), lambda i: (i, 0, 0))
    return pl.pallas_call(
        _mem_kv_body,
        grid=(b,),
        in_specs=[blk(dm), _resident((1, dm)), _resident(w.shape), _resident(gk.shape)],
        out_specs=[blk(CX_WIDTH)] * 2,
        out_shape=[jax.ShapeDtypeStruct((b, mlen, CX_WIDTH), BF16)] * 2,
        compiler_params=_params("parallel"),
        name="mem_kv",
    )(mem, gn, w, gk)


def _merge_body(x_ref, o0_ref, o1_ref, o2_ref, m0_ref, m1_ref, m2_ref, l0_ref, l1_ref, l2_ref, yml_ref, cq_ref,
                g_ref, ck_ref, cv_ref, wa_ref, wm_ref, wc_ref, wout_ref, ex_ref, out_ref, nat_ref):
    tm, dm = x_ref.shape
    nslab = ATT_OUT // LANE

    def natural(ref, dil, width):
        if dil == 1:
            return ref[0, 0].astype(F32)
        m = tm // dil
        for c in range(dil):
            for sl in range(width // LANE):
                nat_ref[sl, pl.ds(c, m, stride=dil), :] = ref[0, c, :, sl * LANE:(sl + 1) * LANE].astype(F32)
        return jnp.concatenate([nat_ref[sl] for sl in range(width // LANE)], axis=1)

    dils = [dil for _, dil in DIL_GROUPS]
    ms = [natural(ref, dil, LANE) for ref, dil in zip((m0_ref, m1_ref, m2_ref), dils)]
    ls = [natural(ref, dil, LANE) for ref, dil in zip((l0_ref, l1_ref, l2_ref), dils)]
    top = jnp.maximum(jnp.maximum(ms[0], ms[1]), ms[2])
    es = [jnp.exp(m - top) for m in ms]
    inv = 1.0 / (es[0] * ls[0] + es[1] * ls[1] + es[2] * ls[2])
    ex = ex_ref[...]

    def widen(a):
        hi = a.astype(BF16)
        lo = (a - hi.astype(F32)).astype(BF16)
        return _dot(jnp.concatenate([hi, lo], axis=1), ex)

    y_att = jnp.zeros((tm, ATT_OUT), F32)
    for e, o_ref, dil in zip(es, (o0_ref, o1_ref, o2_ref), dils):
        y_att = y_att + widen(e * inv) * natural(o_ref, dil, nslab * LANE)

    y_cx = []
    for hd in range(CX_HEADS):
        cs = slice(hd * CX_HEAD_DIM, (hd + 1) * CX_HEAD_DIM)
        s = _dot_nt(cq_ref[:, cs], ck_ref[0, :, cs])
        p = jnp.exp(s - jnp.max(s, axis=-1, keepdims=True))
        l = jnp.sum(p, axis=-1, keepdims=True)
        y_cx.append(_dot(p.astype(BF16), cv_ref[0, :, cs]) / l)
    y_cx = jnp.concatenate(y_cx, axis=1)

    merged = (g_ref[:, 0:dm].astype(F32) * _dot(y_att.astype(BF16), wa_ref[...])
              + g_ref[:, dm:2 * dm].astype(F32) * _dot(yml_ref[...], wm_ref[...])
              + g_ref[:, 2 * dm:3 * dm].astype(F32) * _dot(y_cx.astype(BF16), wc_ref[...]))
    out_ref[...] = x_ref[...] + _dot(merged.astype(BF16), wout_ref[...])


def _merge(x, o_att, m_att, l_att, y_ml, cq, gates, ck, cv, wa, wm, wc, wout, expand, *, seq, tm=512):
    t, dm = x.shape
    mlen = ck.shape[1]
    nseq = seq // tm
    tok = lambda w: pl.BlockSpec((tm, w), lambda i: (i, 0))
    mem = pl.BlockSpec((1, mlen, CX_WIDTH), lambda i: (i // nseq, 0, 0))
    res = lambda dil, w: pl.BlockSpec((1, dil, tm // dil, w), lambda i: (i // nseq, 0, i % nseq, 0))
    dils = [dil for _, dil in DIL_GROUPS]
    return pl.pallas_call(
        _merge_body,
        grid=(t // tm,),
        in_specs=[tok(dm)] + [res(dil, ATT_OUT) for dil in dils] + [res(dil, LANE) for dil in dils] * 2
                 + [tok(ML_WIDTH), tok(CX_WIDTH), tok(N_BRANCH * dm), mem, mem,
                    _resident(wa.shape), _resident(wm.shape), _resident(wc.shape), _resident(wout.shape),
                    _resident(expand.shape)],
        out_specs=tok(dm),
        out_shape=jax.ShapeDtypeStruct((t, dm), F32),
        scratch_shapes=[pltpu.VMEM((ATT_OUT // LANE, tm, LANE), F32)],
        compiler_params=_params("parallel"),
        name="merge",
    )(x, *o_att, *m_att, *l_att, y_ml, cq, gates, ck, cv, wa, wm, wc, wout, expand)


def _rope_tables(seq, tm):
    d = ATT_HEAD_DIM
    inv = ROPE_THETA ** (-jnp.arange(0, d, 2, dtype=F32) / d)
    ang = jnp.arange(seq, dtype=F32)[:, None] * inv[None, :]
    cos, sin = jnp.cos(ang), jnp.sin(ang)
    cos_t = jnp.tile(jnp.concatenate([cos, cos], axis=-1), (1, LANE // d))
    sin_t = jnp.tile(jnp.concatenate([-sin, sin], axis=-1), (1, LANE // d))
    tables = []
    for _, dil in DIL_GROUPS:
        perm = lambda a: a.reshape(seq // tm, tm // dil, dil, -1).swapaxes(1, 2).reshape(seq, -1)
        tables += [perm(cos_t), perm(sin_t)]
    return tables


def _layer(x, mem, p):
    batch, seq, dm = x.shape
    xt = x.reshape(batch * seq, dm)
    row = lambda a: a.reshape(1, -1).astype(F32)
    bf = lambda a: a.astype(BF16)

    x1 = _ffn(xt, row(p["norm_ffn1"]), bf(p["w_ffn1_in"]), bf(p["w_ffn1_out"]), row(p["norm_final"]),
              final_norm=False, tm=1024)

    w_in = bf(p["w_in"])
    sizes = (ATT_QKV, ATT_QKV, ATT_QKV, ML_WIDTH, ML_WIDTH, ML_WIDTH, ML_WIDTH, ML_GATES, CX_WIDTH, N_BRANCH * dm)
    offs = [0]
    for s in sizes:
        offs.append(offs[-1] + s)
    col = lambda i, j=None: w_in[:, offs[i]:offs[(i if j is None else j) + 1]]
    gn = row(p["norm_mix"])

    idx = jnp.arange(MXU_DIM) // ATT_HEAD_DIM
    e64 = jnp.where(idx[:, None] == idx[None, :], 1.0 / ATT_HEAD_DIM, 0.0).astype(BF16)
    tile8 = lambda a: jnp.tile(row(a), (1, ATT_SLOTS))
    qkv = _qkv_proj(x1, gn, col(0), col(1), col(2), tile8(p["att_q_gain"]), tile8(p["att_k_gain"]),
                    _rope_tables(seq, QKV_TILE), e64, batch=batch, seq=seq, tm=QKV_TILE)

    w_if = jnp.pad(col(7), ((0, 0), (0, LANE - ML_GATES)))
    b_if = jnp.pad(row(p["ml_gate_b"]), ((0, 0), (0, LANE - ML_GATES)))
    mq, mkt, mv, og, mif, gates_row, cq, gates = _misc_proj(
        x1, gn, col(3, 4), col(5), col(6), w_if, b_if, col(8), row(p["cx_q_gain"]), col(9),
        row(p["mix_gate_b"]), p["ml_conv_w"].astype(F32), row(p["ml_conv_b"]), batch=batch, seq=seq)

    o_att, m_att, l_att = [], [], []
    for g, (_, dil) in enumerate(DIL_GROUPS):
        o, m, l = _dil_attn(qkv[g], qkv[3 + g], qkv[6 + g], dil=dil)
        o_att.append(o)
        m_att.append(m)
        l_att.append(l)

    y_ml = _mlstm(mq, mkt, mv, og, mif, gates_row, row(p["ml_out_gain"]), batch=batch, seq=seq)

    ck, cv = _mem_kv(mem, row(p["norm_mem"]), bf(p["w_mem_kv"]), row(p["cx_k_gain"]))

    lanes = jnp.arange(ATT_OUT) // ATT_HEAD_DIM * STAT_LANES_PER_HEAD
    expand = jnp.tile((jnp.arange(LANE)[:, None] == lanes[None, :]).astype(BF16), (2, 1))
    x2 = _merge(x1, o_att, m_att, l_att, y_ml, cq, gates, ck, cv, bf(p["w_br_att"]), bf(p["w_br_ml"]),
                bf(p["w_br_cx"]), bf(p["w_out"]), expand, seq=seq)

    x3 = _ffn(x2, row(p["norm_ffn2"]), bf(p["w_ffn2_in"]), bf(p["w_ffn2_out"]), row(p["norm_final"]),
              final_norm=True, tm=512)
    return x3.reshape(batch, seq, dm)


def kernel(x, mem, norm_ffn1, w_ffn1_in, w_ffn1_out, norm_mix, norm_mem, w_in, att_q_gain, att_k_gain,
           ml_conv_w, ml_conv_b, ml_gate_b, ml_out_gain, cx_q_gain, cx_k_gain, w_mem_kv, mix_gate_b,
           w_br_att, w_br_ml, w_br_cx, w_out, norm_ffn2, w_ffn2_in, w_ffn2_out, norm_final):
    params = dict(norm_ffn1=norm_ffn1, w_ffn1_in=w_ffn1_in, w_ffn1_out=w_ffn1_out, norm_mix=norm_mix,
                  norm_mem=norm_mem, w_in=w_in, att_q_gain=att_q_gain, att_k_gain=att_k_gain,
                  ml_conv_w=ml_conv_w, ml_conv_b=ml_conv_b, ml_gate_b=ml_gate_b, ml_out_gain=ml_out_gain,
                  cx_q_gain=cx_q_gain, cx_k_gain=cx_k_gain, w_mem_kv=w_mem_kv, mix_gate_b=mix_gate_b,
                  w_br_att=w_br_att, w_br_ml=w_br_ml, w_br_cx=w_br_cx, w_out=w_out, norm_ffn2=norm_ffn2,
                  w_ffn2_in=w_ffn2_in, w_ffn2_out=w_ffn2_out, norm_final=norm_final)
    depth = norm_ffn1.shape[0]
    for layer in range(depth):
        x = _layer(x, mem, {k: v[layer] for k, v in params.items()})
    return x
```

```python
import functools

import jax
import jax.numpy as jnp
from jax import lax
from jax.experimental import pallas as pl
from jax.experimental.pallas import tpu as pltpu

F32 = jnp.float32
BF16 = jnp.bfloat16

EPS = 1e-6
ROPE_THETA = 10000.0
NEG = -1e30

ATT_HEAD_DIM = 64
ATT_SLOTS = 8
DIL_GROUPS = ((128, 1), (512, 4), (2048, 16))
ATT_OUT = ATT_SLOTS * ATT_HEAD_DIM
ATT_QKV = len(DIL_GROUPS) * ATT_OUT
ML_HEADS = 4
ML_HEAD_DIM = 128
ML_WIDTH = ML_HEADS * ML_HEAD_DIM
ML_GATES = 4 * ML_HEADS
ML_CHUNK = 128
ML_CONV = 5
CX_HEADS = 4
CX_HEAD_DIM = 128
CX_WIDTH = CX_HEADS * CX_HEAD_DIM
N_BRANCH = 3

LANE = 128
MXU_DIM = 256
VMEM_LIMIT = 56 * 1024 * 1024

STAT_LANES_PER_HEAD = LANE // ATT_SLOTS
QKV_TILE = 512
ATTN_KEYS_PER_ITER = 512
BAND = 64
assert all((w // 2) // r == BAND for w, r in DIL_GROUPS)

NT_DIMS = (((1,), (1,)), ((), ()))
TN_DIMS = (((0,), (0,)), ((), ()))


def _dot(a, b):
    return jnp.dot(a, b, preferred_element_type=F32)


def _dot_nt(a, b):
    return lax.dot_general(a, b, NT_DIMS, preferred_element_type=F32)


def _dot_exact(a, b):
    return jnp.dot(a, b, preferred_element_type=F32, precision=lax.Precision.HIGHEST)


def _rms(x):
    return x * lax.rsqrt(jnp.mean(x * x, axis=-1, keepdims=True) + EPS)


def _sigmoid(x):
    return 1.0 / (1.0 + jnp.exp(-x))


def _log_sigmoid(x):
    return jnp.minimum(x, 0.0) - jnp.log(1.0 + jnp.exp(-jnp.abs(x)))


def _params(*sem):
    return pltpu.CompilerParams(dimension_semantics=sem, vmem_limit_bytes=VMEM_LIMIT)


def _resident(shape):
    return pl.BlockSpec(shape, lambda *_: (0,) * len(shape), pipeline_mode=pl.Buffered(1))


def _ffn_body(x_ref, g_ref, wi_ref, wo_ref, gf_ref, o_ref, h_ref, a_ref, *, final_norm, tf):
    h_ref[...] = (_rms(x_ref[...]) * g_ref[...]).astype(BF16)
    d_ff = wo_ref.shape[0]
    for j in range(0, d_ff, tf):
        h = h_ref[...]
        gate = _dot(h, wi_ref[:, j:j + tf])
        up = _dot(h, wi_ref[:, d_ff + j:d_ff + j + tf])
        a_ref[:, j:j + tf] = (gate * _sigmoid(gate) * up).astype(BF16)
    y = x_ref[...] + 0.5 * _dot(a_ref[...], wo_ref[...])
    if final_norm:
        y = _rms(y) * gf_ref[...]
    o_ref[...] = y


def _ffn(x, gain, w_in, w_out, final_gain, *, final_norm, tm, tf=MXU_DIM):
    t, dm = x.shape
    d_ff = w_out.shape[0]
    return pl.pallas_call(
        functools.partial(_ffn_body, final_norm=final_norm, tf=tf),
        grid=(t // tm,),
        in_specs=[pl.BlockSpec((tm, dm), lambda i: (i, 0)), _resident((1, dm)), _resident(w_in.shape),
                  _resident(w_out.shape), _resident((1, dm))],
        out_specs=pl.BlockSpec((tm, dm), lambda i: (i, 0)),
        out_shape=jax.ShapeDtypeStruct((t, dm), F32),
        scratch_shapes=[pltpu.VMEM((tm, dm), BF16), pltpu.VMEM((tm, d_ff), BF16)],
        compiler_params=_params("parallel"),
        name="ffn_final" if final_norm else "ffn",
    )(x, gain, w_in, w_out, final_gain)


def _qkv_body(x_ref, gn_ref, wq_ref, wk_ref, wv_ref, gq_ref, gk_ref, rope_ref, e_ref, *rest):
    outs, (hs_ref, hp_ref) = rest[:3], rest[3:]
    tm, dm = x_ref.shape
    hn = _rms(x_ref[...]) * gn_ref[...]
    for sl in range(dm // LANE):
        hs_ref[sl] = hn[:, sl * LANE:(sl + 1) * LANE]
    lane = lax.broadcasted_iota(jnp.int32, (tm, ATT_OUT), 1)
    first_half = (lane & (ATT_HEAD_DIM - 1)) < ATT_HEAD_DIM // 2
    e = e_ref[...]
    for g, (_, dil) in enumerate(DIL_GROUPS):
        m = tm // dil
        if dil == 1:
            h = hn.astype(BF16)
        else:
            for c in range(dil):
                for sl in range(dm // LANE):
                    hp_ref[c * m:(c + 1) * m, sl * LANE:(sl + 1) * LANE] = (
                        hs_ref[sl, pl.ds(c, m, stride=dil), :].astype(BF16))
            h = hp_ref[...]
        tab = lambda j: jnp.concatenate([rope_ref[:, j * LANE:(j + 1) * LANE]] * (ATT_OUT // LANE), axis=1)
        cos, sin = tab(2 * g), tab(2 * g + 1)
        cs = slice(g * ATT_OUT, (g + 1) * ATT_OUT)

        def emit(slot, val):
            for c in range(dil):
                outs[g][0, c, :, slot * ATT_OUT:(slot + 1) * ATT_OUT] = val[c * m:(c + 1) * m, :].astype(BF16)

        for slot, (w_ref, gain_ref, scale) in enumerate(((wq_ref, gq_ref, ATT_HEAD_DIM ** -0.5),
                                                         (wk_ref, gk_ref, 1.0))):
            t = _dot(h, w_ref[:, cs])
            sq = (t * t).astype(BF16)
            ms = jnp.concatenate([_dot(sq[:, c:c + MXU_DIM], e) for c in range(0, ATT_OUT, MXU_DIM)], axis=1)
            tn = t * lax.rsqrt(ms + EPS) * gain_ref[...]
            partner = jnp.where(first_half,
                                pltpu.roll(tn, ATT_OUT - ATT_HEAD_DIM // 2, 1),
                                pltpu.roll(tn, ATT_HEAD_DIM // 2, 1))
            emit(slot, (tn * cos + partner * sin) * scale)
        emit(2, _dot(h, wv_ref[:, cs]))


def _qkv_proj(x, gn, wq, wk, wv, gq, gk, tables, e64, *, batch, seq, tm=256):
    t, dm = x.shape
    nseq = seq // tm
    pos = pl.BlockSpec((tm, tables.shape[1]), lambda i: (i % nseq, 0))
    out_specs, out_shape = [], []
    for _, dil in DIL_GROUPS:
        out_specs.append(pl.BlockSpec((1, dil, tm // dil, 3 * ATT_OUT), lambda i: (i // nseq, 0, i % nseq, 0)))
        out_shape.append(jax.ShapeDtypeStruct((batch, dil, seq // dil, 3 * ATT_OUT), BF16))
    return pl.pallas_call(
        _qkv_body,
        grid=(t // tm,),
        in_specs=[pl.BlockSpec((tm, dm), lambda i: (i, 0)), _resident((1, dm)), _resident((dm, ATT_QKV)),
                  _resident((dm, ATT_QKV)), _resident((dm, ATT_QKV)), _resident((1, ATT_OUT)),
                  _resident((1, ATT_OUT)), pos, _resident((MXU_DIM, MXU_DIM))],
        out_specs=out_specs,
        out_shape=out_shape,
        scratch_shapes=[pltpu.VMEM((dm // LANE, tm, LANE), F32),
                        pltpu.VMEM((tm, dm), BF16)],
        compiler_params=_params("parallel"),
        name="qkv_proj",
    )(x, gn, wq, wk, wv, gq, gk, tables, e64)


def _misc_body(x_ref, xprev_ref, xnext_ref, gn_ref, wqk_ref, wv_ref, wo_ref, wif_ref, bif_ref, wcq_ref, gcq_ref,
               wg_ref, bg_ref, cw_ref, cb_ref,
               mq_ref, mkt_ref, mv_ref, og_ref, mif_ref, mift_ref, cq_ref, g_ref, hext_ref, pad_ref,
               *, nseq, chunk, row_block=256):
    tm, dm = x_ref.shape
    halo = xprev_ref.shape[0]
    norm = lambda ref: (_rms(ref[...]) * gn_ref[...]).astype(BF16)
    h = norm(x_ref)

    hext_ref[0:halo, :] = norm(xprev_ref)
    hext_ref[halo:halo + tm, :] = h
    hext_ref[halo + tm:, :] = norm(xnext_ref)
    pad_ref[...] = _dot(hext_ref[...], wqk_ref[...])
    pos = pl.program_id(0) % nseq
    pad_ref[0:halo, :] = pad_ref[0:halo, :] * (pos > 0).astype(F32)
    pad_ref[halo + tm:, :] = pad_ref[halo + tm:, :] * (pos < nseq - 1).astype(F32)

    def conv(c):
        cs = slice(c * LANE, (c + 1) * LANE)
        for r0 in range(0, tm, row_block):
            acc = jnp.zeros((row_block, LANE), F32) + cb_ref[:, cs]
            for j in range(ML_CONV):
                acc = acc + cw_ref[j:j + 1, cs] * pad_ref[pl.ds(halo + r0 - ML_CONV // 2 + j, row_block), cs]
            y = acc * _sigmoid(acc)
            if c < ML_HEADS:
                mq_ref[r0:r0 + row_block, cs] = (y * ML_HEAD_DIM ** -0.5).astype(BF16)
            else:
                yt = y.T
                for sub in range(0, row_block, chunk):
                    mkt_ref[0, c - ML_HEADS, (r0 + sub) // chunk] = yt[:, sub:sub + chunk].astype(BF16)

    def small_streams():
        mv_ref[...] = _dot(h, wv_ref[...]).astype(BF16)
        og_ref[...] = _sigmoid(_dot(h, wo_ref[...])).astype(BF16)
        mif = _dot(h, wif_ref[...]) + bif_ref[...]
        mif_ref[...] = mif
        mift_ref[0] = mif.T[:ML_GATES, :]

    def cross_q():
        t = _dot(h, wcq_ref[...])
        for hd in range(CX_HEADS):
            cs = slice(hd * CX_HEAD_DIM, (hd + 1) * CX_HEAD_DIM)
            cq_ref[:, cs] = (_rms(t[:, cs]) * gcq_ref[...] * CX_HEAD_DIM ** -0.5).astype(BF16)

    def mix_gate(cs):
        g_ref[:, cs] = _sigmoid(_dot(h, wg_ref[:, cs]) + bg_ref[:, cs]).astype(BF16)

    half = dm // 2
    matmuls = [small_streams, cross_q] + [functools.partial(mix_gate, slice(o, o + half))
                                          for o in range(0, N_BRANCH * dm, half)]
    assert len(matmuls) == 2 * ML_HEADS
    for c, matmul in enumerate(matmuls):
        matmul()
        conv(c)


def _misc_proj(x, gn, wqk, wv, wo, wif, bif, wcq, gcq, wg, bg, conv_w, conv_b, *, batch, seq, tm=512,
               chunk=ML_CHUNK):
    t, dm = x.shape
    nseq = seq // tm
    halo = 16
    per_halo = tm // halo
    tok = lambda w: pl.BlockSpec((tm, w), lambda i: (i, 0))
    prev = pl.BlockSpec((halo, dm), lambda i: (jnp.maximum(i * per_halo - 1, 0), 0))
    nxt = pl.BlockSpec((halo, dm), lambda i: (jnp.minimum((i + 1) * per_halo, t // halo - 1), 0))
    outs = [(tok(ML_WIDTH), (t, ML_WIDTH), BF16),
            (pl.BlockSpec((1, ML_HEADS, tm // chunk, ML_HEAD_DIM, chunk), lambda i: (i // nseq, 0, i % nseq, 0, 0)),
             (batch, ML_HEADS, seq // chunk, ML_HEAD_DIM, chunk), BF16),
            (tok(ML_WIDTH), (t, ML_WIDTH), BF16),
            (tok(ML_WIDTH), (t, ML_WIDTH), BF16), (tok(LANE), (t, LANE), F32),
            (pl.BlockSpec((1, ML_GATES, tm), lambda i: (i // nseq, 0, i % nseq)), (batch, ML_GATES, seq), F32),
            (tok(CX_WIDTH), (t, CX_WIDTH), BF16), (tok(N_BRANCH * dm), (t, N_BRANCH * dm), BF16)]
    return pl.pallas_call(
        functools.partial(_misc_body, nseq=nseq, chunk=chunk),
        grid=(t // tm,),
        in_specs=[tok(dm), prev, nxt, _resident((1, dm)), _resident(wqk.shape), _resident(wv.shape),
                  _resident(wo.shape), _resident(wif.shape), _resident(bif.shape), _resident(wcq.shape),
                  _resident(gcq.shape), _resident(wg.shape), _resident(bg.shape), _resident(conv_w.shape),
                  _resident(conv_b.shape)],
        out_specs=[spec for spec, _, _ in outs],
        out_shape=[jax.ShapeDtypeStruct(shape, dtype) for _, shape, dtype in outs],
        scratch_shapes=[pltpu.VMEM((tm + 2 * halo, dm), BF16),
                        pltpu.VMEM((tm + 2 * halo, 2 * ML_WIDTH), F32)],
        compiler_params=_params("parallel"),
        name="misc_proj",
    )(x, x, x, gn, wqk, wv, wo, wif, bif, wcq, gcq, wg, bg, conv_w, conv_b)


def _attn_body(qkv_ref, o_ref, stat_ref, *, dil, n, bq, win, unroll):
    lane = lax.broadcasted_iota(jnp.int32, (bq, LANE), 1)
    low_head = lane < ATT_HEAD_DIM
    stat_group = lane // STAT_LANES_PER_HEAD
    delta = (lax.broadcasted_iota(jnp.int32, (bq, win), 0)
             - lax.broadcasted_iota(jnp.int32, (bq, win), 1))
    ones = jnp.ones((win, LANE), BF16)

    def where_block(i, u):
        blk = i * unroll + u
        c = blk // nblk
        q0 = pl.multiple_of((blk % nblk) * bq, bq)
        ks = pl.multiple_of(jnp.clip(q0 - BAND, 0, n - win), BAND)
        valid = lax.bitcast_convert_type(delta + (q0 - ks + BAND), jnp.uint32) <= jnp.uint32(2 * BAND)
        return c, q0, ks, valid

    def scores(c, q0, ks, valid, pair):
        qp = qkv_ref[0, c, pl.ds(q0, bq), pair * LANE:(pair + 1) * LANE]
        kp = qkv_ref[0, c, pl.ds(ks, win), ATT_OUT + pair * LANE:ATT_OUT + (pair + 1) * LANE]
        return [jnp.where(valid, _dot_nt(jnp.where(sel, qp, jnp.zeros_like(qp)), kp), NEG)
                for sel in (low_head, ~low_head)]

    def finish(c, q0, ks, _, pair, s_pair, stats):
        cs = slice(pair * LANE, (pair + 1) * LANE)
        vp = qkv_ref[0, c, pl.ds(ks, win), 2 * ATT_OUT + pair * LANE:2 * ATT_OUT + (pair + 1) * LANE]
        vext = jnp.concatenate([vp, ones], axis=1)
        m_row, l_row = stats
        halves = []
        for half, s in enumerate(s_pair):
            m = jnp.max(s, axis=-1, keepdims=True)
            p = jnp.exp((s - m).astype(BF16))
            pv = _dot(p, vext)
            halves.append(pv[:, :LANE])
            here = stat_group == 2 * pair + half
            m_row = jnp.where(here, m, m_row)
            l_row = jnp.where(here, pv[:, LANE:], l_row)
        o_ref[0, c, pl.ds(q0, bq), cs] = jnp.where(low_head, halves[0], halves[1]).astype(BF16)
        return m_row, l_row

    nblk = n // bq
    npair = ATT_SLOTS // 2
    depth = 4

    def step(i, carry):
        blocks = [where_block(i, u) for u in range(unroll)]
        tasks = [(*block, pair) for block in blocks for pair in range(npair)]
        ahead = [scores(*task) for task in tasks[:depth]]
        zeros = jnp.zeros((bq, LANE), F32)
        stats = None
        for t, task in enumerate(tasks):
            if t + depth < len(tasks):
                ahead.append(scores(*tasks[t + depth]))
            stats = finish(*task, ahead.pop(0), (zeros, zeros) if task[-1] == 0 else stats)
            if task[-1] == npair - 1:
                stat_ref[0, task[0], pl.ds(task[1], bq), 0:LANE] = stats[0]
                stat_ref[0, task[0], pl.ds(task[1], bq), LANE:2 * LANE] = stats[1]
        return carry
    lax.fori_loop(0, dil * nblk // unroll, step, 0)


def _dil_attn(qkv, *, dil, bq=128):
    batch, _, n, _ = qkv.shape
    bq = min(bq, n)
    win = min(n, bq + 2 * BAND)
    unroll = max(1, ATTN_KEYS_PER_ITER // win)
    blk = lambda w: pl.BlockSpec((1, dil, n, w), lambda b: (b, 0, 0, 0))
    return pl.pallas_call(
        functools.partial(_attn_body, dil=dil, n=n, bq=bq, win=win, unroll=unroll),
        grid=(batch,),
        in_specs=[blk(3 * ATT_OUT)],
        out_specs=[blk(ATT_OUT), blk(2 * LANE)],
        out_shape=[jax.ShapeDtypeStruct((batch, dil, n, ATT_OUT), BF16),
                   jax.ShapeDtypeStruct((batch, dil, n, 2 * LANE), F32)],
        compiler_params=_params("parallel"),
        name=f"dil_attn_r{dil}",
    )(qkv)


def _mlstm_body(q_ref, kt_ref, v_ref, og_ref, gc_ref, gr_ref, gain_ref, o_ref,
                hf_ref, hb_ref, cn_ref, m_ref, bcol_ref, brow_ref, gtot_ref, *, seq, chunk, row_block=256):
    assert chunk == LANE
    nrb = seq // row_block
    cn_ref[...] = jnp.zeros_like(cn_ref)
    m_ref[...] = jnp.zeros_like(m_ref)

    nc = seq // chunk
    ri = lax.broadcasted_iota(jnp.int32, (chunk, chunk), 0)
    ci = lax.broadcasted_iota(jnp.int32, (chunk, chunk), 1)
    lower = ri >= ci
    upper = ri <= ci
    lower_f = lower.astype(F32)
    upper_f = upper.astype(F32)

    ones = jnp.ones((chunk, LANE), BF16)

    bwd_lane = lax.broadcasted_iota(jnp.int32, (chunk, LANE), 1) >= 2 * ML_HEADS
    bwd_sublane = lax.broadcasted_iota(jnp.int32, (ML_GATES, chunk), 0) >= 2 * ML_HEADS

    lf_cols = [_log_sigmoid(gc_ref[0, c * chunk:(c + 1) * chunk, :]) for c in range(nc)]
    pre_cols = _dot_exact(lower_f, jnp.concatenate(lf_cols, axis=1))
    for c in range(nc):
        pre = pre_cols[:, c * LANE:(c + 1) * LANE]
        bcol_ref[c * chunk:(c + 1) * chunk, :] = jnp.where(bwd_lane, pre[chunk - 1:chunk, :] - pre + lf_cols[c], pre)
    lf_rows = [_log_sigmoid(gr_ref[0, :, c * chunk:(c + 1) * chunk]) for c in range(nc)]
    pre_rows = _dot_exact(jnp.concatenate(lf_rows, axis=0), upper_f)
    for c in range(nc):
        pre = pre_rows[c * ML_GATES:(c + 1) * ML_GATES, :]
        tot = pre[:, chunk - 1:chunk]
        brow_ref[c] = jnp.where(bwd_sublane, tot - pre + lf_rows[c], pre)
        gtot_ref[c] = jnp.broadcast_to(tot, (ML_GATES, chunk))

    def step(it, carry):
        units = []
        for d in range(2):
            cidx = it if d == 0 else nc - 1 - it
            rows = pl.ds(pl.multiple_of(cidx * chunk, chunk), chunk)
            bcol = bcol_ref[rows, :]
            brow = brow_ref[cidx]
            grow = gr_ref[0, :, rows]
            gtot = gtot_ref[cidx]
            for hd in range(ML_HEADS):
                i_gate = (2 * d) * ML_HEADS + hd
                f_gate = (2 * d + 1) * ML_HEADS + hd
                units.append(dict(
                    st=d * ML_HEADS + hd, d=d, hd=hd, cidx=cidx, rows=rows,
                    hs=slice(hd * LANE, (hd + 1) * LANE),
                    b_col=bcol[:, f_gate:f_gate + 1],
                    w_row=brow[f_gate:f_gate + 1, :] - grow[i_gate:i_gate + 1, :],
                    gtot=gtot[f_gate:f_gate + 1, :]))

        for u in units:
            u["q"] = q_ref[0, u["rows"], u["hs"]]
            u["kt"] = kt_ref[0, u["hd"], u["cidx"]]
            u["s"] = _dot(u["q"], u["kt"])
            u["qcn"] = _dot(u["q"], cn_ref[u["st"]].astype(BF16))
        for u in units:
            u["m_row"] = m_ref[u["st"]][0:1, :]
            m_prev = jnp.max(u["m_row"], axis=-1, keepdims=True)
            g = jnp.where(lower if u["d"] == 0 else upper, -u["w_row"], NEG)
            rm = jnp.maximum(jnp.max(g, axis=-1, keepdims=True), m_prev)
            sc = u["s"] * jnp.exp(g - rm)
            u["vext"] = jnp.concatenate([v_ref[0, u["rows"], u["hs"]], ones], axis=1)
            u["nd"] = _dot(sc.astype(BF16), u["vext"])
            u["w_inter"] = jnp.exp(m_prev - rm)
            u["floor"] = jnp.exp(-(u["b_col"] + rm))
        for u in units:
            a_row = u["gtot"] - u["w_row"]
            m_new = jnp.maximum(u["gtot"] + u["m_row"], jnp.max(a_row, axis=-1, keepdims=True))
            decay = jnp.exp(u["gtot"] + u["m_row"] - m_new)
            ktw = (u["kt"].astype(F32) * jnp.exp(a_row - m_new)).astype(BF16)
            cn_ref[u["st"]] = (jnp.concatenate([decay, decay], axis=1) * cn_ref[u["st"]]
                               + _dot(ktw, u["vext"]))
            m_ref[u["st"]] = jnp.broadcast_to(m_new, m_ref.shape[1:])
        for u in units:
            tot = u["nd"] + u["w_inter"] * u["qcn"]
            hout = tot[:, :LANE] / jnp.maximum(jnp.abs(tot[:, LANE:]), u["floor"])
            if u["d"] == 0:
                hf_ref[u["rows"], u["hs"]] = hout
            else:
                hb_ref[u["rows"], u["hs"]] = hout
        return carry

    lax.fori_loop(0, nc, step, 0)

    def finish(i, carry):
        rows = pl.ds(pl.multiple_of(i * row_block, row_block), row_block)
        for hd in range(ML_HEADS):
            hs = slice(hd * LANE, (hd + 1) * LANE)
            hm = _rms(hf_ref[rows, hs] + hb_ref[rows, hs]) * gain_ref[:, hs]
            o_ref[0, rows, hs] = (hm * og_ref[0, rows, hs].astype(F32)).astype(BF16)
        return carry
    lax.fori_loop(0, nrb, finish, 0)


def _mlstm(mq, mkt, mv, og, gates_col, gates_row, out_gain, *, batch, seq, chunk=ML_CHUNK):
    b3 = lambda a: a.reshape(batch, seq, a.shape[-1])
    per_b = lambda w: pl.BlockSpec((1, seq, w), lambda b: (b, 0, 0))
    y = pl.pallas_call(
        functools.partial(_mlstm_body, seq=seq, chunk=chunk),
        grid=(batch,),
        in_specs=[per_b(ML_WIDTH), pl.BlockSpec((1,) + mkt.shape[1:], lambda b: (b, 0, 0, 0, 0)),
                  per_b(ML_WIDTH), per_b(ML_WIDTH), per_b(LANE),
                  pl.BlockSpec((1, ML_GATES, seq), lambda b: (b, 0, 0)), _resident(out_gain.shape)],
        out_specs=per_b(ML_WIDTH),
        out_shape=jax.ShapeDtypeStruct((batch, seq, ML_WIDTH), BF16),
        scratch_shapes=[
            pltpu.VMEM((seq, ML_WIDTH), F32),
            pltpu.VMEM((seq, ML_WIDTH), F32),
            pltpu.VMEM((2 * ML_HEADS, ML_HEAD_DIM, 2 * LANE), F32),
            pltpu.VMEM((2 * ML_HEADS, 8, LANE), F32),
            pltpu.VMEM((seq, LANE), F32),
            pltpu.VMEM((seq // chunk, ML_GATES, chunk), F32),
            pltpu.VMEM((seq // chunk, ML_GATES, chunk), F32),
        ],
        compiler_params=_params("parallel"),
        name="mlstm",
    )(b3(mq), mkt, b3(mv), b3(og), b3(gates_col), gates_row, out_gain)
    return y.reshape(batch * seq, ML_WIDTH)


def _mem_kv_body(mem_ref, gn_ref, w_ref, gk_ref, ck_ref, cv_ref):
    h = (_rms(mem_ref[0]) * gn_ref[...]).astype(BF16)
    kv = _dot(h, w_ref[...])
    for hd in range(CX_HEADS):
        cs = slice(hd * CX_HEAD_DIM, (hd + 1) * CX_HEAD_DIM)
        ck_ref[0, :, cs] = (_rms(kv[:, cs]) * gk_ref[...]).astype(BF16)
    cv_ref[0] = kv[:, CX_WIDTH:].astype(BF16)


def _mem_kv(mem, gn, w, gk):
    b, mlen, dm = mem.shape
    blk = lambda w_: pl.BlockSpec((1, mlen, w_), lambda i: (i, 0, 0))
    return pl.pallas_call(
        _mem_kv_body,
        grid=(b,),
        in_specs=[blk(dm), _resident((1, dm)), _resident(w.shape), _resident(gk.shape)],
        out_specs=[blk(CX_WIDTH)] * 2,
        out_shape=[jax.ShapeDtypeStruct((b, mlen, CX_WIDTH), BF16)] * 2,
        compiler_params=_params("parallel"),
        name="mem_kv",
    )(mem, gn, w, gk)


def _merge_body(x_ref, o0_ref, o1_ref, o2_ref, st0_ref, st1_ref, st2_ref, yml_ref, cq_ref,
                g_ref, ck_ref, cv_ref, wa_ref, wm_ref, wc_ref, wout_ref, ex_ref, out_ref, nat_ref):
    tm, dm = x_ref.shape
    nslab = ATT_OUT // LANE

    def natural(ref, dil, width):
        if dil == 1:
            return ref[0, 0].astype(F32)
        m = tm // dil
        for c in range(dil):
            for sl in range(width // LANE):
                nat_ref[sl, pl.ds(c, m, stride=dil), :] = ref[0, c, :, sl * LANE:(sl + 1) * LANE].astype(F32)
        return jnp.concatenate([nat_ref[sl] for sl in range(width // LANE)], axis=1)

    dils = [dil for _, dil in DIL_GROUPS]
    stats = [natural(ref, dil, 2 * LANE) for ref, dil in zip((st0_ref, st1_ref, st2_ref), dils)]
    ms = [st[:, :LANE] for st in stats]
    ls = [st[:, LANE:] for st in stats]
    top = jnp.maximum(jnp.maximum(ms[0], ms[1]), ms[2])
    es = [jnp.exp(m - top) for m in ms]
    inv = 1.0 / (es[0] * ls[0] + es[1] * ls[1] + es[2] * ls[2])
    ex = ex_ref[...]

    def widen(a):
        hi = a.astype(BF16)
        lo = (a - hi.astype(F32)).astype(BF16)
        return _dot(jnp.concatenate([hi, lo], axis=1), ex)

    y_att = jnp.zeros((tm, ATT_OUT), F32)
    for e, o_ref, dil in zip(es, (o0_ref, o1_ref, o2_ref), dils):
        y_att = y_att + widen(e * inv) * natural(o_ref, dil, nslab * LANE)

    y_cx = []
    for hd in range(CX_HEADS):
        cs = slice(hd * CX_HEAD_DIM, (hd + 1) * CX_HEAD_DIM)
        s = _dot_nt(cq_ref[:, cs], ck_ref[0, :, cs])
        p = jnp.exp(s - jnp.max(s, axis=-1, keepdims=True))
        l = jnp.sum(p, axis=-1, keepdims=True)
        y_cx.append(_dot(p.astype(BF16), cv_ref[0, :, cs]) / l)
    y_cx = jnp.concatenate(y_cx, axis=1)

    merged = (g_ref[:, 0:dm].astype(F32) * _dot(y_att.astype(BF16), wa_ref[...])
              + g_ref[:, dm:2 * dm].astype(F32) * _dot(yml_ref[...], wm_ref[...])
              + g_ref[:, 2 * dm:3 * dm].astype(F32) * _dot(y_cx.astype(BF16), wc_ref[...]))
    out_ref[...] = x_ref[...] + _dot(merged.astype(BF16), wout_ref[...])


def _merge(x, o_att, stat_att, y_ml, cq, gates, ck, cv, wa, wm, wc, wout, expand, *, seq, tm=512):
    t, dm = x.shape
    mlen = ck.shape[1]
    nseq = seq // tm
    tok = lambda w: pl.BlockSpec((tm, w), lambda i: (i, 0))
    mem = pl.BlockSpec((1, mlen, CX_WIDTH), lambda i: (i // nseq, 0, 0))
    res = lambda dil, w: pl.BlockSpec((1, dil, tm // dil, w), lambda i: (i // nseq, 0, i % nseq, 0))
    dils = [dil for _, dil in DIL_GROUPS]
    return pl.pallas_call(
        _merge_body,
        grid=(t // tm,),
        in_specs=[tok(dm)] + [res(dil, ATT_OUT) for dil in dils] + [res(dil, 2 * LANE) for dil in dils]
                 + [tok(ML_WIDTH), tok(CX_WIDTH), tok(N_BRANCH * dm), mem, mem,
                    _resident(wa.shape), _resident(wm.shape), _resident(wc.shape), _resident(wout.shape),
                    _resident(expand.shape)],
        out_specs=tok(dm),
        out_shape=jax.ShapeDtypeStruct((t, dm), F32),
        scratch_shapes=[pltpu.VMEM((ATT_OUT // LANE, tm, LANE), F32)],
        compiler_params=_params("parallel"),
        name="merge",
    )(x, *o_att, *stat_att, y_ml, cq, gates, ck, cv, wa, wm, wc, wout, expand)


def _rope_tables(seq, tm):
    d = ATT_HEAD_DIM
    inv = ROPE_THETA ** (-jnp.arange(0, d, 2, dtype=F32) / d)
    ang = jnp.arange(seq, dtype=F32)[:, None] * inv[None, :]
    cos, sin = jnp.cos(ang), jnp.sin(ang)
    cos_t = jnp.tile(jnp.concatenate([cos, cos], axis=-1), (1, LANE // d))
    sin_t = jnp.tile(jnp.concatenate([-sin, sin], axis=-1), (1, LANE // d))
    tables = []
    for _, dil in DIL_GROUPS:
        perm = lambda a: a.reshape(seq // tm, tm // dil, dil, -1).swapaxes(1, 2).reshape(seq, -1)
        tables += [perm(cos_t), perm(sin_t)]
    return jnp.concatenate(tables, axis=1)


def _layer(x, mem, p):
    batch, seq, dm = x.shape
    xt = x.reshape(batch * seq, dm)
    row = lambda a: a.reshape(1, -1).astype(F32)
    bf = lambda a: a.astype(BF16)

    x1 = _ffn(xt, row(p["norm_ffn1"]), bf(p["w_ffn1_in"]), bf(p["w_ffn1_out"]), row(p["norm_final"]),
              final_norm=False, tm=1024)

    w_in = bf(p["w_in"])
    sizes = (ATT_QKV, ATT_QKV, ATT_QKV, ML_WIDTH, ML_WIDTH, ML_WIDTH, ML_WIDTH, ML_GATES, CX_WIDTH, N_BRANCH * dm)
    offs = [0]
    for s in sizes:
        offs.append(offs[-1] + s)
    col = lambda i, j=None: w_in[:, offs[i]:offs[(i if j is None else j) + 1]]
    gn = row(p["norm_mix"])

    idx = jnp.arange(MXU_DIM) // ATT_HEAD_DIM
    e64 = jnp.where(idx[:, None] == idx[None, :], 1.0 / ATT_HEAD_DIM, 0.0).astype(BF16)
    tile8 = lambda a: jnp.tile(row(a), (1, ATT_SLOTS))
    qkv = _qkv_proj(x1, gn, col(0), col(1), col(2), tile8(p["att_q_gain"]), tile8(p["att_k_gain"]),
                    _rope_tables(seq, QKV_TILE), e64, batch=batch, seq=seq, tm=QKV_TILE)

    w_if = jnp.pad(col(7), ((0, 0), (0, LANE - ML_GATES)))
    b_if = jnp.pad(row(p["ml_gate_b"]), ((0, 0), (0, LANE - ML_GATES)))
    mq, mkt, mv, og, mif, gates_row, cq, gates = _misc_proj(
        x1, gn, col(3, 4), col(5), col(6), w_if, b_if, col(8), row(p["cx_q_gain"]), col(9),
        row(p["mix_gate_b"]), p["ml_conv_w"].astype(F32), row(p["ml_conv_b"]), batch=batch, seq=seq)

    o_att, stat_att = [], []
    for g, (_, dil) in enumerate(DIL_GROUPS):
        o, stat = _dil_attn(qkv[g], dil=dil)
        o_att.append(o)
        stat_att.append(stat)

    y_ml = _mlstm(mq, mkt, mv, og, mif, gates_row, row(p["ml_out_gain"]), batch=batch, seq=seq)

    ck, cv = _mem_kv(mem, row(p["norm_mem"]), bf(p["w_mem_kv"]), row(p["cx_k_gain"]))

    lanes = jnp.arange(ATT_OUT) // ATT_HEAD_DIM * STAT_LANES_PER_HEAD
    expand = jnp.tile((jnp.arange(LANE)[:, None] == lanes[None, :]).astype(BF16), (2, 1))
    x2 = _merge(x1, o_att, stat_att, y_ml, cq, gates, ck, cv, bf(p["w_br_att"]), bf(p["w_br_ml"]),
                bf(p["w_br_cx"]), bf(p["w_out"]), expand, seq=seq)

    x3 = _ffn(x2, row(p["norm_ffn2"]), bf(p["w_ffn2_in"]), bf(p["w_ffn2_out"]), row(p["norm_final"]),
              final_norm=True, tm=512)
    return x3.reshape(batch, seq, dm)


def kernel(x, mem, norm_ffn1, w_ffn1_in, w_ffn1_out, norm_mix, norm_mem, w_in, att_q_gain, att_k_gain,
           ml_conv_w, ml_conv_b, ml_gate_b, ml_out_gain, cx_q_gain, cx_k_gain, w_mem_kv, mix_gate_b,
           w_br_att, w_br_ml, w_br_cx, w_out, norm_ffn2, w_ffn2_in, w_ffn2_out, norm_final):
    params = dict(norm_ffn1=norm_ffn1, w_ffn1_in=w_ffn1_in, w_ffn1_out=w_ffn1_out, norm_mix=norm_mix,
                  norm_mem=norm_mem, w_in=w_in, att_q_gain=att_q_gain, att_k_gain=att_k_gain,
                  ml_conv_w=ml_conv_w, ml_conv_b=ml_conv_b, ml_gate_b=ml_gate_b, ml_out_gain=ml_out_gain,
                  cx_q_gain=cx_q_gain, cx_k_gain=cx_k_gain, w_mem_kv=w_mem_kv, mix_gate_b=mix_gate_b,
                  w_br_att=w_br_att, w_br_ml=w_br_ml, w_br_cx=w_br_cx, w_out=w_out, norm_ffn2=norm_ffn2,
                  w_ffn2_in=w_ffn2_in, w_ffn2_out=w_ffn2_out, norm_final=norm_final)
    depth = norm_ffn1.shape[0]
    for layer in range(depth):
        x = _layer(x, mem, {k: v[layer] for k, v in params.items()})
    return x
```
